```python
import math
import jax
import jax.numpy as jnp
from jax import lax
import numpy as np

D_MODEL = 1024
BATCH = 16
SEQ = 4096
DEPTH = 4
DEC_BATCH = 2
DEC_SEQ = 16384
PAST_LEN = 128

HEAD_DIM = 128
ATT_PATTERNS = ((128, 1), (512, 4), (2048, 16))
HEADS_PER_GROUP = 4
ATT_HEADS = HEADS_PER_GROUP * len(ATT_PATTERNS)
ATT_W = ATT_HEADS * HEAD_DIM
ATT_OUT = HEADS_PER_GROUP * HEAD_DIM
ROPE_THETA = 10000.0

EXPAND = 2
D_INNER = EXPAND * D_MODEL
SSM_HEAD_DIM = 64
SSM_HEADS = D_INNER // SSM_HEAD_DIM
SSM_GROUPS = 4
D_STATE = 128
CONV_K = 5
CONV_PAD = CONV_K // 2
CONV_CH = D_INNER + 2 * SSM_GROUPS * D_STATE
CHUNK = 128

IN_COLS = 3 * ATT_W + D_INNER + CONV_CH + 2 * SSM_HEADS + 2 * D_MODEL

D_FF = ((8 * D_MODEL // 3 + 255) // 256) * 256

NORM_EPS = 1e-6

kernel_name = "hybrid_dilated_attn_ssd_encoder"


def rms_norm(x, w):
    xf = x.astype(jnp.float32)
    y = xf * lax.rsqrt(jnp.mean(xf * xf, axis=-1, keepdims=True) + NORM_EPS)
    return (y * w.astype(jnp.float32)).astype(x.dtype)


def rope(x, pos):
    half = HEAD_DIM // 2
    inv = ROPE_THETA ** (-jnp.arange(half, dtype=jnp.float32) / half)
    ang = pos.astype(jnp.float32)[:, None] * inv[None, :]
    cos = jnp.cos(ang)[None, :, None, :]
    sin = jnp.sin(ang)[None, :, None, :]
    xf = x.astype(jnp.float32)
    x1, x2 = xf[..., :half], xf[..., half:]
    return jnp.concatenate([x1 * cos - x2 * sin, x2 * cos + x1 * sin], axis=-1).astype(x.dtype)


def dilated_window_attention(q, k, v, radius, dilation):
    b, l, hg, hd = q.shape
    n = l // dilation
    R = radius

    def split(t):
        return t.reshape(b, n, dilation, hg, hd).transpose(0, 2, 3, 1, 4)

    qs, ks, vs = split(q), split(k), split(v)
    nb = -(-n // R)
    n_pad = nb * R
    qp = jnp.pad(qs, [(0, 0)] * 3 + [(0, n_pad - n), (0, 0)])
    kv_pad = [(0, 0)] * 3 + [(R, n_pad - n + R), (0, 0)]
    kp = jnp.pad(ks, kv_pad).reshape(b, dilation, hg, nb + 2, R, hd)
    vp = jnp.pad(vs, kv_pad).reshape(b, dilation, hg, nb + 2, R, hd)
    qb = qp.reshape(b, dilation, hg, nb, R, hd)
    kb = jnp.concatenate([kp[:, :, :, 0:nb], kp[:, :, :, 1:nb + 1], kp[:, :, :, 2:nb + 2]], axis=-2)
    vb = jnp.concatenate([vp[:, :, :, 0:nb], vp[:, :, :, 1:nb + 1], vp[:, :, :, 2:nb + 2]], axis=-2)
    blk = jnp.arange(nb)[:, None, None]
    s_idx = jnp.arange(R)[None, :, None]
    t_idx = jnp.arange(3 * R)[None, None, :]
    kpos = (blk - 1) * R + t_idx
    off = t_idx - R - s_idx
    mask = (jnp.abs(off) <= R) & (kpos >= 0) & (kpos < n)
    scale = 1.0 / math.sqrt(hd)
    sc = jnp.einsum("bdhiqe,bdhike->bdhiqk", qb, kb).astype(jnp.float32) * scale
    sc = jnp.where(mask, sc, -jnp.inf)
    m = jnp.max(sc, axis=-1, keepdims=True)
    e = jnp.exp(sc - m)
    den = jnp.sum(e, axis=-1, keepdims=True)
    out = jnp.einsum("bdhiqk,bdhike->bdhiqe", e, vb.astype(jnp.float32)) / den
    lse = (m + jnp.log(den))[..., 0]
    out = out.reshape(b, dilation, hg, n_pad, hd)[:, :, :, :n]
    out = out.transpose(0, 3, 1, 2, 4).reshape(b, l, hg, hd)
    lse = lse.reshape(b, dilation, hg, n_pad)[:, :, :, :n].transpose(0, 3, 1, 2).reshape(b, l, hg)
    return out.astype(q.dtype), lse


def attention_branch(q, k, v):
    b, l, _ = q.shape
    pos = jnp.arange(l)
    q = rope(q.reshape(b, l, ATT_HEADS, HEAD_DIM), pos)
    k = rope(k.reshape(b, l, ATT_HEADS, HEAD_DIM), pos)
    v = v.reshape(b, l, ATT_HEADS, HEAD_DIM)
    outs, lses = [], []
    for gi, (window, dil) in enumerate(ATT_PATTERNS):
        sl = slice(gi * HEADS_PER_GROUP, (gi + 1) * HEADS_PER_GROUP)
        radius = window // (2 * dil)
        o, s = dilated_window_attention(q[:, :, sl], k[:, :, sl], v[:, :, sl], radius, dil)
        outs.append(o)
        lses.append(s)
    alpha = jax.nn.softmax(jnp.stack(lses, axis=0), axis=0)
    o = jnp.einsum("gblh,gblhe->blhe", alpha, jnp.stack(outs, axis=0).astype(jnp.float32))
    return o.reshape(b, l, ATT_OUT).astype(q.dtype)


def depthwise_conv(x, w, bias):
    y = lax.conv_general_dilated(
        x, w[:, None, :], window_strides=(1,), padding=[(CONV_PAD, CONV_PAD)],
        dimension_numbers=("NWC", "WIO", "NWC"), feature_group_count=x.shape[-1])
    return y + bias


def ssd_chunked(x, dt, a, bm, cm):
    b, l, h, p = x.shape
    g, n = bm.shape[-2], bm.shape[-1]
    r = h // g
    c = l // CHUNK
    dtf = dt.astype(jnp.float32)
    xc = (x.astype(jnp.float32) * dtf[..., None]).reshape(b, c, CHUNK, g, r, p)
    adt = (dtf * a.astype(jnp.float32)).reshape(b, c, CHUNK, g, r)
    acum = jnp.cumsum(adt, axis=2)
    bc = bm.astype(jnp.float32).reshape(b, c, CHUNK, g, n)
    cc = cm.astype(jnp.float32).reshape(b, c, CHUNK, g, n)
    tril = jnp.tril(jnp.ones((CHUNK, CHUNK), dtype=bool))
    seg = acum[:, :, :, None] - acum[:, :, None, :]
    lmat = jnp.exp(jnp.where(tril[None, None, :, :, None, None], seg, -jnp.inf))
    cb = jnp.einsum("bclgn,bcsgn->bclsg", cc, bc)
    y_diag = jnp.einsum("bclsg,bclsgr,bcsgrp->bclgrp", cb, lmat, xc)
    decay = jnp.exp(acum[:, :, -1:] - acum)
    states = jnp.einsum("bclgn,bclgr,bclgrp->bcgrpn", bc, decay, xc)
    chunk_decay = jnp.exp(acum[:, :, -1])

    def step(carry, inp):
        st, dec = inp
        return carry * dec[..., None, None] + st, carry

    _, prev = lax.scan(step, jnp.zeros_like(states[:, 0]),
                       (jnp.moveaxis(states, 1, 0), jnp.moveaxis(chunk_decay, 1, 0)))
    prev = jnp.moveaxis(prev, 0, 1)
    y_off = jnp.einsum("bclgn,bcgrpn,bclgr->bclgrp", cc, prev, jnp.exp(acum))
    return (y_diag + y_off).reshape(b, l, h, p).astype(x.dtype)


def ssm_branch(z, xbc, dt_raw, conv_w, conv_b, a_log, dt_bias, d_skip, norm_w):
    b, l, _ = z.shape
    xbc = jax.nn.silu(depthwise_conv(xbc, conv_w, conv_b))
    xs = xbc[..., :D_INNER].reshape(b, l, SSM_HEADS, SSM_HEAD_DIM)
    bm = xbc[..., D_INNER:D_INNER + SSM_GROUPS * D_STATE].reshape(b, l, SSM_GROUPS, D_STATE)
    cm = xbc[..., D_INNER + SSM_GROUPS * D_STATE:].reshape(b, l, SSM_GROUPS, D_STATE)
    dt = jax.nn.softplus(dt_raw.reshape(b, l, 2, SSM_HEADS).astype(jnp.float32)
                         + dt_bias.astype(jnp.float32))
    a = -jnp.exp(a_log.astype(jnp.float32))
    y_fwd = ssd_chunked(xs, dt[:, :, 0], a[0], bm, cm)
    flip = lambda t: jnp.flip(t, axis=1)
    y_bwd = flip(ssd_chunked(flip(xs), flip(dt[:, :, 1]), a[1], flip(bm), flip(cm)))
    y = y_fwd + y_bwd + xs * d_skip[:, None]
    y = y.reshape(b, l, D_INNER) * jax.nn.silu(z)
    yf = y.astype(jnp.float32).reshape(b, l, SSM_GROUPS, D_INNER // SSM_GROUPS)
    yf = yf * lax.rsqrt(jnp.mean(yf * yf, axis=-1, keepdims=True) + NORM_EPS)
    return (yf.reshape(b, l, D_INNER) * norm_w.astype(jnp.float32)).astype(z.dtype)


def encoder_layer(x, norm_mix, w_in, conv_w, conv_b, a_log, dt_bias, d_skip, ssm_norm,
                  w_attn_out, w_ssm_out, w_out, norm_ffn, w_gate_up, w_down):
    h = rms_norm(x, norm_mix)
    proj = h @ w_in
    bounds = np.cumsum([ATT_W, ATT_W, ATT_W, D_INNER, CONV_CH, 2 * SSM_HEADS, D_MODEL])
    q, k, v, z, xbc, dt_raw, g_a, g_m = jnp.split(proj, [int(i) for i in bounds], axis=-1)
    att = attention_branch(q, k, v)
    ssm = ssm_branch(z, xbc, dt_raw, conv_w, conv_b, a_log, dt_bias, d_skip, ssm_norm)
    mixed = jax.nn.sigmoid(g_a) * (att @ w_attn_out) + jax.nn.sigmoid(g_m) * (ssm @ w_ssm_out)
    x = x + mixed @ w_out
    h2 = rms_norm(x, norm_ffn)
    gu = h2 @ w_gate_up
    gate, up = gu[..., :D_FF], gu[..., D_FF:]
    return x + (jax.nn.silu(gate) * up) @ w_down


def setup_inputs(seed: int = 0) -> dict:
    key = jax.random.key(seed)
    ks = jax.random.split(key, 20)
    f32 = jnp.float32
    nrm = lambda k, shape, s: jax.random.normal(k, shape, f32) * s
    dt0 = jnp.exp(jax.random.uniform(ks[6], (DEPTH, 2, SSM_HEADS), f32,
                                     math.log(1e-3), math.log(1e-1)))
    return {
        "x_prompt": nrm(ks[0], (BATCH, SEQ, D_MODEL), 1.0),
        "x_sample": nrm(ks[1], (DEC_BATCH, DEC_SEQ, D_MODEL), 1.0),
        "norm_mix": 1.0 + nrm(ks[2], (DEPTH, D_MODEL), 0.01),
        "w_in": nrm(ks[3], (DEPTH, D_MODEL, IN_COLS), D_MODEL ** -0.5),
        "conv_w": nrm(ks[4], (DEPTH, CONV_K, CONV_CH), CONV_K ** -0.5),
        "conv_b": nrm(ks[5], (DEPTH, CONV_CH), 0.01),
        "a_log": jnp.log(jax.random.uniform(ks[7], (DEPTH, 2, SSM_HEADS), f32, 1.0, 16.0)),
        "dt_bias": dt0 + jnp.log(-jnp.expm1(-dt0)),
        "d_skip": 1.0 + nrm(ks[8], (DEPTH, SSM_HEADS), 0.01),
        "ssm_norm": 1.0 + nrm(ks[9], (DEPTH, D_INNER), 0.01),
        "w_attn_out": nrm(ks[10], (DEPTH, ATT_OUT, D_MODEL), ATT_OUT ** -0.5),
        "w_ssm_out": nrm(ks[11], (DEPTH, D_INNER, D_MODEL), D_INNER ** -0.5),
        "w_out": nrm(ks[12], (DEPTH, D_MODEL, D_MODEL), D_MODEL ** -0.5),
        "norm_ffn": 1.0 + nrm(ks[13], (DEPTH, D_MODEL), 0.01),
        "w_gate_up": nrm(ks[14], (DEPTH, D_MODEL, 2 * D_FF), D_MODEL ** -0.5),
        "w_down": nrm(ks[15], (DEPTH, D_FF, D_MODEL), D_FF ** -0.5),
        "norm_final": 1.0 + nrm(ks[16], (D_MODEL,), 0.01),
    }


def reference(x_prompt, x_sample, norm_mix, w_in, conv_w, conv_b, a_log, dt_bias, d_skip,
              ssm_norm, w_attn_out, w_ssm_out, w_out, norm_ffn, w_gate_up, w_down, norm_final):
    def trunk(x):
        for i in range(DEPTH):
            x = encoder_layer(x, norm_mix[i], w_in[i], conv_w[i], conv_b[i], a_log[i], dt_bias[i],
                              d_skip[i], ssm_norm[i], w_attn_out[i], w_ssm_out[i], w_out[i],
                              norm_ffn[i], w_gate_up[i], w_down[i])
        return rms_norm(x, norm_final)

    y_prompt = trunk(x_prompt)
    y_sample = trunk(x_sample)
    return (y_prompt, y_sample)
```

```python
import functools
import math

import jax
import jax.numpy as jnp
import numpy as np
from jax import lax
from jax.experimental import pallas as pl
from jax.experimental.pallas import tpu as pltpu

D_MODEL = 1024
DEPTH = 4
HEAD_DIM = 128
ATT_PATTERNS = ((128, 1), (512, 4), (2048, 16))
HEADS_PER_GROUP = 4
ATT_GROUPS = len(ATT_PATTERNS)
ATT_W = HEADS_PER_GROUP * ATT_GROUPS * HEAD_DIM
ATT_OUT = HEADS_PER_GROUP * HEAD_DIM
ROPE_THETA = 10000.0
D_INNER = 2048
SSM_HEAD_DIM = 64
SSM_HEADS = D_INNER // SSM_HEAD_DIM
SSM_GROUPS = 4
D_STATE = 128
CONV_K = 5
CONV_PAD = CONV_K // 2
CONV_CH = D_INNER + 2 * SSM_GROUPS * D_STATE
D_FF = 2816
NORM_EPS = 1e-6

MXU_DTYPE = jnp.bfloat16
LANES = 128
SUBLANES = 8
SSD_CHUNK = 128
GROUP_W = D_INNER // SSM_GROUPS
HEADS_PER_SSM_GROUP = SSM_HEADS // SSM_GROUPS
ATT_TQ = 128

COL_Z = 0
COL_GA = COL_Z + D_INNER
COL_GM = COL_GA + D_MODEL
COL_Q = COL_GM + D_MODEL
COL_K = COL_Q + ATT_W
COL_V = COL_K + ATT_W
COL_DT = COL_V + ATT_W
DT_PAD_W = 512
COL_XBC = COL_DT + DT_PAD_W
PROJ_W = COL_XBC + CONV_CH
PROJ_TN = 512
ROPE_J0 = COL_Q // PROJ_TN
ROPE_J1 = COL_V // PROJ_TN

VMEM_LIMIT = 56 * 1024 * 1024


def _mxu(x):
    return x.astype(MXU_DTYPE)


def _sigmoid(x):
    return 1.0 / (1.0 + jnp.exp(-x))


def _split_dot(lhs01, x):
    hi = _mxu(x)
    r1 = x - hi.astype(jnp.float32)
    mid = _mxu(r1)
    lo = _mxu(r1 - mid.astype(jnp.float32))
    d = lambda p: jnp.dot(lhs01, p, preferred_element_type=jnp.float32)
    return d(hi) + d(mid) + d(lo)


def _split_dot_r(x, rhs01):
    hi = _mxu(x)
    r1 = x - hi.astype(jnp.float32)
    mid = _mxu(r1)
    lo = _mxu(r1 - mid.astype(jnp.float32))
    d = lambda p: jnp.dot(p, rhs01, preferred_element_type=jnp.float32)
    return d(hi) + d(mid) + d(lo)


def _inproj_kernel(x_ref, nw_ref, w_ref, cos_ref, sin_ref, o_ref, h_ref):
    j = pl.program_id(1)

    @pl.when(j == 0)
    def _():
        x = x_ref[...]
        ms = jnp.mean(x * x, axis=-1, keepdims=True)
        h_ref[...] = _mxu(x * lax.rsqrt(ms + NORM_EPS) * nw_ref[...])

    acc = jnp.dot(h_ref[...], w_ref[...], preferred_element_type=jnp.float32)
    is_rope = jnp.logical_and(j >= ROPE_J0, j < ROPE_J1)

    @pl.when(is_rope)
    def _():
        cos = cos_ref[...]
        sin = sin_ref[...]
        for hh in range(PROJ_TN // HEAD_DIM):
            sl = slice(hh * HEAD_DIM, (hh + 1) * HEAD_DIM)
            t = acc[:, sl]
            o_ref[:, sl] = t * cos + pltpu.roll(t, HEAD_DIM // 2, axis=1) * sin

    @pl.when(jnp.logical_not(is_rope))
    def _():
        o_ref[...] = acc


def _inproj(xf, norm_w, w, cos2, sin2, seq_len):
    t_total = xf.shape[0]
    tm = min(1024, seq_len)
    pos_blocks = seq_len // tm
    return pl.pallas_call(
        _inproj_kernel,
        grid=(t_total // tm, PROJ_W // PROJ_TN),
        in_specs=[
            pl.BlockSpec((tm, D_MODEL), lambda i, j: (i, 0)),
            pl.BlockSpec((1, D_MODEL), lambda i, j: (0, 0)),
            pl.BlockSpec((D_MODEL, PROJ_TN), lambda i, j: (0, j)),
            pl.BlockSpec((tm, HEAD_DIM), lambda i, j: (i % pos_blocks, 0)),
            pl.BlockSpec((tm, HEAD_DIM), lambda i, j: (i % pos_blocks, 0)),
        ],
        out_specs=pl.BlockSpec((tm, PROJ_TN), lambda i, j: (i, j)),
        out_shape=jax.ShapeDtypeStruct((t_total, PROJ_W), jnp.float32),
        scratch_shapes=[pltpu.VMEM((tm, D_MODEL), MXU_DTYPE)],
        compiler_params=pltpu.CompilerParams(
            dimension_semantics=("parallel", "arbitrary"), vmem_limit_bytes=VMEM_LIMIT),
        name="inproj",
    )(xf, norm_w, w, cos2, sin2)


def _attn_kernel(q_ref, k_ref, v_ref, o_ref, lse_ref, *, n, tq, win, radius):
    i = pl.program_id(1)
    start = pl.multiple_of(jnp.clip(i * tq - radius, 0, n - win), 64)
    q = q_ref[0]
    kw = k_ref[0, pl.ds(start, win), :]
    vw = v_ref[0, pl.ds(start, win), :]
    s = lax.dot_general(q, kw, (((1,), (1,)), ((), ())), preferred_element_type=jnp.float32)
    s = s * (1.0 / math.sqrt(HEAD_DIM))
    qpos = i * tq + lax.broadcasted_iota(jnp.int32, (tq, win), 0)
    kpos = start + lax.broadcasted_iota(jnp.int32, (tq, win), 1)
    s = jnp.where(jnp.abs(kpos - qpos) <= radius, s, -jnp.inf)
    m = jnp.max(s, axis=-1, keepdims=True)
    e = jnp.exp(s - m)
    den = jnp.sum(e, axis=-1, keepdims=True)
    o_ref[0] = jnp.dot(_mxu(e), vw, preferred_element_type=jnp.float32) / den
    lse_ref[0] = m + jnp.log(den)


def _attn_class(q, k, v, radius):
    s_total, n, _ = q.shape
    tq = min(ATT_TQ, n)
    win = min(n, tq + 2 * radius)
    kern = functools.partial(_attn_kernel, n=n, tq=tq, win=win, radius=radius)
    return pl.pallas_call(
        kern,
        grid=(s_total, n // tq),
        in_specs=[
            pl.BlockSpec((1, tq, HEAD_DIM), lambda s, i: (s, i, 0)),
            pl.BlockSpec((1, n, HEAD_DIM), lambda s, i: (s, 0, 0)),
            pl.BlockSpec((1, n, HEAD_DIM), lambda s, i: (s, 0, 0)),
        ],
        out_specs=[
            pl.BlockSpec((1, tq, HEAD_DIM), lambda s, i: (s, i, 0)),
            pl.BlockSpec((1, tq, 1), lambda s, i: (s, i, 0)),
        ],
        out_shape=[
            jax.ShapeDtypeStruct((s_total, n, HEAD_DIM), jnp.float32),
            jax.ShapeDtypeStruct((s_total, n, 1), jnp.float32),
        ],
        compiler_params=pltpu.CompilerParams(
            dimension_semantics=("parallel", "arbitrary"), vmem_limit_bytes=VMEM_LIMIT),
        name="band_attn",
    )(q, k, v)


def _attention(proj, batch, seq_len):
    t_total = batch * seq_len
    outs, lses = [], []
    for g, (window, dil) in enumerate(ATT_PATTERNS):
        radius = window // (2 * dil)
        n = seq_len // dil

        def to_classes(col0):
            t = proj[:, col0 + g * ATT_OUT: col0 + (g + 1) * ATT_OUT]
            t = t.reshape(batch, n, dil, HEADS_PER_GROUP, HEAD_DIM).transpose(0, 2, 3, 1, 4)
            return _mxu(t.reshape(batch * dil * HEADS_PER_GROUP, n, HEAD_DIM))

        o, lse = _attn_class(to_classes(COL_Q), to_classes(COL_K), to_classes(COL_V), radius)
        o = o.reshape(batch, dil, HEADS_PER_GROUP, n, HEAD_DIM).transpose(0, 3, 1, 2, 4)
        outs.append(o.reshape(t_total, ATT_OUT))
        lse = lse.reshape(batch, dil, HEADS_PER_GROUP, n).transpose(0, 3, 1, 2)
        lses.append(lse.reshape(t_total, HEADS_PER_GROUP))
    return outs, jnp.concatenate(lses, axis=1)


def _ssd_kernel(*refs, rev, n_chunks):
    if rev:
        (cur_ref, prev_ref, next_ref, dt_ref, z_ref, yf_ref, cw_ref, cbias_ref, dtb_ref,
         alog_ref, e_ref, nw_ref, o_ref, st_ref, xe_ref, xbc_ref) = refs
    else:
        (cur_ref, prev_ref, next_ref, dt_ref, cw_ref, cbias_ref, dtb_ref,
         alog_ref, e_ref, dskip_ref, o_ref, st_ref, xe_ref, xbc_ref) = refs
    q = SSD_CHUNK
    c = pl.program_id(1)
    t = (n_chunks - 1 - c) if rev else c

    @pl.when(c == 0)
    def _():
        st_ref[...] = jnp.zeros_like(st_ref)

    xe_ref[0:SUBLANES, :] = jnp.where(t > 0, prev_ref[...], 0.0)
    xe_ref[SUBLANES:SUBLANES + q, :] = cur_ref[...]
    xe_ref[SUBLANES + q:, :] = jnp.where(t < n_chunks - 1, next_ref[...], 0.0)
    for cc in range(CONV_CH // GROUP_W):
        sl = slice(cc * GROUP_W, (cc + 1) * GROUP_W)
        acc = cbias_ref[:, sl]
        for k in range(CONV_K):
            acc = acc + cw_ref[k:k + 1, sl] * xe_ref[pl.ds(SUBLANES - CONV_PAD + k, q), sl]
        xbc_ref[:, sl] = acc * _sigmoid(acc)

    v = dt_ref[...] + dtb_ref[...]
    dt = jnp.maximum(v, 0.0) + jnp.log1p(jnp.exp(-jnp.abs(v)))
    adt = dt * (-jnp.exp(alog_ref[...]))
    row = lax.broadcasted_iota(jnp.int32, (q, q), 0)
    col = lax.broadcasted_iota(jnp.int32, (q, q), 1)
    inside = (col >= row) if rev else (col <= row)
    acum = _split_dot(jnp.where(inside, 1.0, 0.0).astype(MXU_DTYPE), adt)
    tot = acum[0:1, :] if rev else acum[q - 1:q, :]
    acum_t = acum.T
    dt_t = dt.T
    e_mat = e_ref[...]
    ex_acc = _split_dot_r(jnp.exp(acum), e_mat)
    ex_w = _split_dot_r(jnp.exp(tot - acum) * dt, e_mat)
    dec_row = ex_acc[0:1, :] if rev else ex_acc[q - 1:q, :]
    lane = lax.broadcasted_iota(jnp.int32, (q, LANES), 1)
    lane_off = SSM_HEADS if rev else 0

    for g in range(SSM_GROUPS):
        gsl = slice(g * GROUP_W, (g + 1) * GROUP_W)
        b_f32 = xbc_ref[:, D_INNER + g * D_STATE: D_INNER + (g + 1) * D_STATE]
        c_g = _mxu(xbc_ref[:, D_INNER + GROUP_W + g * D_STATE: D_INNER + GROUP_W + (g + 1) * D_STATE])
        cb = lax.dot_general(c_g, _mxu(b_f32), (((1,), (1,)), ((), ())),
                             preferred_element_type=jnp.float32)
        x_g = xbc_ref[:, gsl]
        y_g = jnp.dot(c_g, _mxu(st_ref[:, gsl]), preferred_element_type=jnp.float32) * ex_acc[:, gsl]
        parts = []
        for jj in range(HEADS_PER_SSM_GROUP // 2):
            ms = []
            for h in (g * HEADS_PER_SSM_GROUP + 2 * jj, g * HEADS_PER_SSM_GROUP + 2 * jj + 1):
                hl = lane_off + h
                seg = acum[:, hl:hl + 1] - acum_t[hl:hl + 1, :]
                lm = jnp.exp(jnp.where(inside, seg, -jnp.inf))
                ms.append(cb * lm * dt_t[hl:hl + 1, :])
            m_pair = _mxu(jnp.concatenate(ms, axis=1))
            xp = x_g[:, jj * LANES:(jj + 1) * LANES]
            r_pair = _mxu(jnp.concatenate(
                [jnp.where(lane < SSM_HEAD_DIM, xp, 0.0), jnp.where(lane >= SSM_HEAD_DIM, xp, 0.0)],
                axis=0))
            parts.append(jnp.dot(m_pair, r_pair, preferred_element_type=jnp.float32))
        y_g = y_g + jnp.concatenate(parts, axis=1)
        s_new = jnp.dot(_mxu(b_f32.T), _mxu(x_g * ex_w[:, gsl]), preferred_element_type=jnp.float32)
        st_ref[:, gsl] = st_ref[:, gsl] * dec_row[:, gsl] + s_new
        if rev:
            y = yf_ref[:, gsl] + y_g
            zz = z_ref[:, gsl]
            y = y * (zz * _sigmoid(zz))
            ms_ = jnp.mean(y * y, axis=-1, keepdims=True)
            o_ref[:, gsl] = (y * lax.rsqrt(ms_ + NORM_EPS) * nw_ref[:, gsl]).astype(o_ref.dtype)
        else:
            o_ref[:, gsl] = y_g + x_g * dskip_ref[:, gsl]


def _ssd_sweep(proj, yf, conv_w8, conv_b, dtb, alog, e_mat, row_w, batch, seq_len, rev):
    t_total = batch * seq_len
    q = SSD_CHUNK
    nc = seq_len // q
    blk8 = q // SUBLANES
    last8 = t_total // SUBLANES - 1
    tmap = (lambda c: nc - 1 - c) if rev else (lambda c: c)
    xbc_blk = COL_XBC // CONV_CH
    rowmap = lambda b, c: b * nc + tmap(c)
    const = lambda shape: pl.BlockSpec(shape, lambda b, c: (0, 0))
    data_specs = [
        pl.BlockSpec((q, CONV_CH), lambda b, c: (rowmap(b, c), xbc_blk)),
        pl.BlockSpec((SUBLANES, CONV_CH),
                     lambda b, c: (jnp.maximum(rowmap(b, c) * blk8 - 1, 0), xbc_blk)),
        pl.BlockSpec((SUBLANES, CONV_CH),
                     lambda b, c: (jnp.minimum((rowmap(b, c) + 1) * blk8, last8), xbc_blk)),
        pl.BlockSpec((q, LANES), lambda b, c: (rowmap(b, c), COL_DT // LANES)),
    ]
    args = [proj, proj, proj, proj]
    if rev:
        data_specs += [
            pl.BlockSpec((q, D_INNER), lambda b, c: (rowmap(b, c), COL_Z // D_INNER)),
            pl.BlockSpec((q, D_INNER), lambda b, c: (rowmap(b, c), 0)),
        ]
        args += [proj, yf]
    param_specs = [const((SUBLANES, CONV_CH)), const((1, CONV_CH)), const((1, LANES)),
                   const((1, LANES)), const((LANES, D_INNER)), const((1, D_INNER))]
    if rev:
        args += [conv_w8, conv_b, dtb, alog, e_mat, row_w]
    else:
        args += [conv_w8, conv_b, dtb, alog, e_mat, row_w]
    kern = functools.partial(_ssd_kernel, rev=rev, n_chunks=nc)
    return pl.pallas_call(
        kern,
        grid=(batch, nc),
        in_specs=data_specs + param_specs,
        out_specs=pl.BlockSpec((q, D_INNER), lambda b, c: (rowmap(b, c), 0)),
        out_shape=jax.ShapeDtypeStruct((t_total, D_INNER), MXU_DTYPE if rev else jnp.float32),
        scratch_shapes=[
            pltpu.VMEM((D_STATE, D_INNER), jnp.float32),
            pltpu.VMEM((q + 2 * SUBLANES, CONV_CH), jnp.float32),
            pltpu.VMEM((q, CONV_CH), jnp.float32),
        ],
        compiler_params=pltpu.CompilerParams(
            dimension_semantics=("parallel", "arbitrary"), vmem_limit_bytes=VMEM_LIMIT),
        name="ssd_bwd" if rev else "ssd_fwd",
    )(*args)


def _mix_kernel(x_ref, a0_ref, a1_ref, a2_ref, lse_ref, ssm_ref, ga_ref, gm_ref,
                wao_ref, wso_ref, wout_ref, o_ref):
    lse = lse_ref[...]
    hg = HEADS_PER_GROUP
    l0, l1, l2 = lse[:, 0:hg], lse[:, hg:2 * hg], lse[:, 2 * hg:3 * hg]
    m = jnp.maximum(jnp.maximum(l0, l1), l2)
    e0, e1, e2 = jnp.exp(l0 - m), jnp.exp(l1 - m), jnp.exp(l2 - m)
    den = e0 + e1 + e2
    al0, al1, al2 = e0 / den, e1 / den, e2 / den
    heads = []
    for h in range(hg):
        sl = slice(h * HEAD_DIM, (h + 1) * HEAD_DIM)
        heads.append(al0[:, h:h + 1] * a0_ref[:, sl] + al1[:, h:h + 1] * a1_ref[:, sl]
                     + al2[:, h:h + 1] * a2_ref[:, sl])
    att = jnp.concatenate(heads, axis=1)
    pa = jnp.dot(_mxu(att), wao_ref[...], preferred_element_type=jnp.float32)
    ps = jnp.dot(ssm_ref[...], wso_ref[...], preferred_element_type=jnp.float32)
    mixed = _sigmoid(ga_ref[...]) * pa + _sigmoid(gm_ref[...]) * ps
    o_ref[...] = x_ref[...] + jnp.dot(_mxu(mixed), wout_ref[...], preferred_element_type=jnp.float32)


def _mix(xf, att_outs, lse, ssm, proj, wao, wso, wout):
    t_total = xf.shape[0]
    tm = 512
    row = lambda w: pl.BlockSpec((tm, w), lambda i: (i, 0))
    const = lambda a: pl.BlockSpec(a.shape, lambda i: (0, 0), pipeline_mode=pl.Buffered(1))
    return pl.pallas_call(
        _mix_kernel,
        grid=(t_total // tm,),
        in_specs=[row(D_MODEL), row(ATT_OUT), row(ATT_OUT), row(ATT_OUT),
                  row(HEADS_PER_GROUP * ATT_GROUPS), row(D_INNER),
                  pl.BlockSpec((tm, D_MODEL), lambda i: (i, COL_GA // D_MODEL)),
                  pl.BlockSpec((tm, D_MODEL), lambda i: (i, COL_GM // D_MODEL)),
                  const(wao), const(wso), const(wout)],
        out_specs=row(D_MODEL),
        out_shape=jax.ShapeDtypeStruct((t_total, D_MODEL), jnp.float32),
        compiler_params=pltpu.CompilerParams(
            dimension_semantics=("parallel",), vmem_limit_bytes=VMEM_LIMIT),
        name="mix_out",
    )(xf, att_outs[0], att_outs[1], att_outs[2], lse, ssm, proj, proj, wao, wso, wout)


FFN_CHUNK = 1408


def _ffn_kernel(x_ref, nw_ref, wgu_ref, wd_ref, fw_ref, o_ref, *, final_norm):
    x = x_ref[...]
    ms = jnp.mean(x * x, axis=-1, keepdims=True)
    h = _mxu(x * lax.rsqrt(ms + NORM_EPS) * nw_ref[...])
    acc = x
    for c0 in range(0, D_FF, FFN_CHUNK):
        gate = jnp.dot(h, wgu_ref[:, c0:c0 + FFN_CHUNK], preferred_element_type=jnp.float32)
        up = jnp.dot(h, wgu_ref[:, D_FF + c0:D_FF + c0 + FFN_CHUNK],
                     preferred_element_type=jnp.float32)
        a = _mxu(gate * _sigmoid(gate) * up)
        acc = acc + jnp.dot(a, wd_ref[c0:c0 + FFN_CHUNK, :], preferred_element_type=jnp.float32)
    if final_norm:
        ms2 = jnp.mean(acc * acc, axis=-1, keepdims=True)
        acc = acc * lax.rsqrt(ms2 + NORM_EPS) * fw_ref[...]
    o_ref[...] = acc


def _ffn(x1, norm_w, wgu, wd, final_w, final_norm):
    t_total = x1.shape[0]
    tm = 512
    row = pl.BlockSpec((tm, D_MODEL), lambda i: (i, 0))
    const = lambda a: pl.BlockSpec(a.shape, lambda i: (0, 0), pipeline_mode=pl.Buffered(1))
    return pl.pallas_call(
        functools.partial(_ffn_kernel, final_norm=final_norm),
        grid=(t_total // tm,),
        in_specs=[row, const(norm_w), const(wgu), const(wd), const(final_w)],
        out_specs=row,
        out_shape=jax.ShapeDtypeStruct((t_total, D_MODEL), jnp.float32),
        compiler_params=pltpu.CompilerParams(
            dimension_semantics=("parallel",), vmem_limit_bytes=VMEM_LIMIT),
        name="ffn",
    )(x1, norm_w, wgu, wd, final_w)


def _prep_w_in(w_in):
    b = [int(v) for v in np.cumsum([ATT_W, ATT_W, ATT_W, D_INNER, CONV_CH, 2 * SSM_HEADS, D_MODEL])]
    q, k, v = w_in[..., :b[0]], w_in[..., b[0]:b[1]], w_in[..., b[1]:b[2]]
    z, xbc, dt = w_in[..., b[2]:b[3]], w_in[..., b[3]:b[4]], w_in[..., b[4]:b[5]]
    ga, gm = w_in[..., b[5]:b[6]], w_in[..., b[6]:]
    pad = jnp.zeros(w_in.shape[:-1] + (DT_PAD_W - 2 * SSM_HEADS,), w_in.dtype)
    return _mxu(jnp.concatenate([z, ga, gm, q, k, v, dt, pad, xbc], axis=-1))


def _pad_lanes(a, width):
    return jnp.pad(a, [(0, 0)] * (a.ndim - 1) + [(0, width - a.shape[-1])])


def _rope_tables(seq_len):
    half = HEAD_DIM // 2
    inv = ROPE_THETA ** (-jnp.arange(half, dtype=jnp.float32) / half)
    ang = jnp.arange(seq_len).astype(jnp.float32)[:, None] * inv[None, :]
    cos, sin = jnp.cos(ang), jnp.sin(ang)
    return jnp.concatenate([cos, cos], axis=1), jnp.concatenate([-sin, sin], axis=1)


def _expand_matrix(row0):
    r = np.zeros((LANES, D_INNER), np.float32)
    for h in range(SSM_HEADS):
        r[row0 + h, h * SSM_HEAD_DIM:(h + 1) * SSM_HEAD_DIM] = 1.0
    return jnp.asarray(r, MXU_DTYPE)


def _trunk(x, p):
    batch, seq_len, _ = x.shape
    xf = x.reshape(batch * seq_len, D_MODEL)
    cos2, sin2 = _rope_tables(seq_len)
    for i in range(DEPTH):
        proj = _inproj(xf, p["norm_mix"][i], p["w_in"][i], cos2, sin2, seq_len)
        att_outs, lse = _attention(proj, batch, seq_len)
        ssd_args = (p["conv_w"][i], p["conv_b"][i], p["dt_bias"][i], p["a_log"][i])
        yf = _ssd_sweep(proj, None, *ssd_args, p["e_fwd"], p["d_skip"][i], batch, seq_len, False)
        ssm = _ssd_sweep(proj, yf, *ssd_args, p["e_bwd"], p["ssm_norm"][i], batch, seq_len, True)
        x1 = _mix(xf, att_outs, lse, ssm, proj, p["w_attn_out"][i], p["w_ssm_out"][i], p["w_out"][i])
        xf = _ffn(x1, p["norm_ffn"][i], p["w_gate_up"][i], p["w_down"][i], p["norm_final"],
                  i == DEPTH - 1)
    return xf.reshape(batch, seq_len, D_MODEL)


def _prep_params(norm_mix, w_in, conv_w, conv_b, a_log, dt_bias, d_skip, ssm_norm,
                 w_attn_out, w_ssm_out, w_out, norm_ffn, w_gate_up, w_down, norm_final):
    return {
        "norm_mix": norm_mix[:, None, :],
        "w_in": _prep_w_in(w_in),
        "conv_w": jnp.pad(conv_w, ((0, 0), (0, SUBLANES - CONV_K), (0, 0))),
        "conv_b": conv_b[:, None, :],
        "a_log": _pad_lanes(a_log.reshape(DEPTH, 1, 2 * SSM_HEADS), LANES),
        "dt_bias": _pad_lanes(dt_bias.reshape(DEPTH, 1, 2 * SSM_HEADS), LANES),
        "d_skip": jnp.repeat(d_skip, SSM_HEAD_DIM, axis=-1)[:, None, :],
        "ssm_norm": ssm_norm[:, None, :],
        "w_attn_out": _mxu(w_attn_out),
        "w_ssm_out": _mxu(w_ssm_out),
        "w_out": _mxu(w_out),
        "norm_ffn": norm_ffn[:, None, :],
        "w_gate_up": _mxu(w_gate_up),
        "w_down": _mxu(w_down),
        "norm_final": norm_final[None, :],
        "e_fwd": _expand_matrix(0),
        "e_bwd": _expand_matrix(SSM_HEADS),
    }


def kernel(x_prompt, x_sample, norm_mix, w_in, conv_w, conv_b, a_log, dt_bias, d_skip, ssm_norm,
           w_attn_out, w_ssm_out, w_out, norm_ffn, w_gate_up, w_down, norm_final):
    p = _prep_params(norm_mix, w_in, conv_w, conv_b, a_log, dt_bias, d_skip, ssm_norm,
                     w_attn_out, w_ssm_out, w_out, norm_ffn, w_gate_up, w_down, norm_final)
    return (_trunk(x_prompt, p), _trunk(x_sample, p))
```

```python
import functools
import math

import jax
import jax.numpy as jnp
import numpy as np
from jax import lax
from jax.experimental import pallas as pl
from jax.experimental.pallas import tpu as pltpu

D_MODEL = 1024
DEPTH = 4
HEAD_DIM = 128
ATT_PATTERNS = ((128, 1), (512, 4), (2048, 16))
HEADS_PER_GROUP = 4
ATT_GROUPS = len(ATT_PATTERNS)
ATT_W = HEADS_PER_GROUP * ATT_GROUPS * HEAD_DIM
ATT_OUT = HEADS_PER_GROUP * HEAD_DIM
ROPE_THETA = 10000.0
D_INNER = 2048
SSM_HEAD_DIM = 64
SSM_HEADS = D_INNER // SSM_HEAD_DIM
SSM_GROUPS = 4
D_STATE = 128
CONV_K = 5
CONV_PAD = CONV_K // 2
BC_W = 2 * SSM_GROUPS * D_STATE
CONV_CH = D_INNER + BC_W
D_FF = 2816
NORM_EPS = 1e-6

MXU_DTYPE = jnp.bfloat16
LANES = 128
SUBLANES = 8
PACKED_ROWS = 16
SSD_CHUNK = 128
GROUP_W = D_INNER // SSM_GROUPS
HEADS_PER_SSM_GROUP = SSM_HEADS // SSM_GROUPS
ATT_SUB = 128
QKV_W = 3 * ATT_OUT

VMEM_LIMIT = 56 * 1024 * 1024


def _mxu(x):
    return x.astype(MXU_DTYPE)


def _sigmoid(x):
    return 1.0 / (1.0 + jnp.exp(-x))


def _split3(x):
    hi = _mxu(x)
    r1 = x - hi.astype(jnp.float32)
    mid = _mxu(r1)
    return hi, mid, _mxu(r1 - mid.astype(jnp.float32))


def _split_dot(lhs01, x):
    return sum(jnp.dot(lhs01, p, preferred_element_type=jnp.float32) for p in _split3(x))


def _split_dot_r(x, rhs01):
    return sum(jnp.dot(p, rhs01, preferred_element_type=jnp.float32) for p in _split3(x))


def _rms_normed(x, w):
    ms = jnp.mean(x * x, axis=-1, keepdims=True)
    return _mxu(x * lax.rsqrt(ms + NORM_EPS) * w)


def _const_spec(a, grid_rank):
    zeros = (0,) * a.ndim
    if grid_rank == 1:
        return pl.BlockSpec(a.shape, lambda i: zeros, pipeline_mode=pl.Buffered(1))
    return pl.BlockSpec(a.shape, lambda i, j: zeros, pipeline_mode=pl.Buffered(1))


def _proj_gates_kernel(x_ref, nw_ref, w_ref, o_ref, h_ref):
    j = pl.program_id(1)

    @pl.when(j == 0)
    def _():
        h_ref[...] = _rms_normed(x_ref[...], nw_ref[...])

    acc = jnp.dot(h_ref[...], w_ref[...], preferred_element_type=jnp.float32)
    s = _sigmoid(acc)
    o_ref[...] = jnp.where(j == 0, acc * s, s).astype(o_ref.dtype)


def _proj_gates(xf, norm_w, w):
    t_total = xf.shape[0]
    tm, tn = 1024, D_INNER
    return pl.pallas_call(
        _proj_gates_kernel,
        grid=(t_total // tm, w.shape[1] // tn),
        in_specs=[
            pl.BlockSpec((tm, D_MODEL), lambda i, j: (i, 0)),
            _const_spec(norm_w, 2),
            pl.BlockSpec((D_MODEL, tn), lambda i, j: (0, j)),
        ],
        out_specs=pl.BlockSpec((tm, tn), lambda i, j: (i, j)),
        out_shape=jax.ShapeDtypeStruct((t_total, w.shape[1]), MXU_DTYPE),
        scratch_shapes=[pltpu.VMEM((tm, D_MODEL), MXU_DTYPE)],
        compiler_params=pltpu.CompilerParams(
            dimension_semantics=("parallel", "arbitrary"), vmem_limit_bytes=VMEM_LIMIT),
        name="proj_gates",
    )(xf, norm_w, w)


def _proj_xbc_kernel(x_ref, xp_ref, xn_ref, nw_ref, w_ref, wdt_ref, cw_ref, cbias_ref, dtb_ref,
                     xs_ref, bc_ref, dt_ref, h_ref, acc_ref, *, tiles_per_seq):
    tm = x_ref.shape[0]
    hr = PACKED_ROWS
    iseq = pl.program_id(0) % tiles_per_seq
    nw = nw_ref[...]
    h_ref[0:hr, :] = _rms_normed(xp_ref[...], nw)
    h_ref[hr:hr + tm, :] = _rms_normed(x_ref[...], nw)
    h_ref[hr + tm:, :] = _rms_normed(xn_ref[...], nw)
    acc_ref[...] = jnp.dot(h_ref[...], w_ref[...], preferred_element_type=jnp.float32)
    acc_ref[0:hr, :] = jnp.where(iseq > 0, acc_ref[0:hr, :], 0.0)
    acc_ref[hr + tm:, :] = jnp.where(iseq < tiles_per_seq - 1, acc_ref[hr + tm:, :], 0.0)
    for cc in range(CONV_CH // GROUP_W):
        sl = slice(cc * GROUP_W, (cc + 1) * GROUP_W)
        a = cbias_ref[:, sl]
        for k in range(CONV_K):
            a = a + cw_ref[k:k + 1, sl] * acc_ref[pl.ds(hr - CONV_PAD + k, tm), sl]
        y = a * _sigmoid(a)
        if cc < D_INNER // GROUP_W:
            xs_ref[:, sl] = y
        else:
            bc_ref[:, cc * GROUP_W - D_INNER:(cc + 1) * GROUP_W - D_INNER] = _mxu(y)
    v = jnp.dot(h_ref[hr:hr + tm, :], wdt_ref[...], preferred_element_type=jnp.float32) + dtb_ref[...]
    dt_ref[...] = jnp.maximum(v, 0.0) + jnp.log1p(jnp.exp(-jnp.abs(v)))


def _proj_xbc(xf, norm_w, w, wdt, conv_w8, conv_b, dtb, seq_len):
    t_total = xf.shape[0]
    tm = 512
    hr = PACKED_ROWS
    per = tm // hr
    last = t_total // hr - 1
    row = lambda width: pl.BlockSpec((tm, width), lambda i: (i, 0))
    kern = functools.partial(_proj_xbc_kernel, tiles_per_seq=seq_len // tm)
    return pl.pallas_call(
        kern,
        grid=(t_total // tm,),
        in_specs=[
            row(D_MODEL),
            pl.BlockSpec((hr, D_MODEL), lambda i: (jnp.maximum(i * per - 1, 0), 0)),
            pl.BlockSpec((hr, D_MODEL), lambda i: (jnp.minimum((i + 1) * per, last), 0)),
            _const_spec(norm_w, 1), _const_spec(w, 1), _const_spec(wdt, 1),
            _const_spec(conv_w8, 1), _const_spec(conv_b, 1), _const_spec(dtb, 1),
        ],
        out_specs=[row(D_INNER), row(BC_W), row(LANES)],
        out_shape=[
            jax.ShapeDtypeStruct((t_total, D_INNER), jnp.float32),
            jax.ShapeDtypeStruct((t_total, BC_W), MXU_DTYPE),
            jax.ShapeDtypeStruct((t_total, LANES), jnp.float32),
        ],
        scratch_shapes=[
            pltpu.VMEM((tm + 2 * hr, D_MODEL), MXU_DTYPE),
            pltpu.VMEM((tm + 2 * hr, CONV_CH), jnp.float32),
        ],
        compiler_params=pltpu.CompilerParams(
            dimension_semantics=("parallel",), vmem_limit_bytes=VMEM_LIMIT),
        name="proj_xbc",
    )(xf, xf, xf, norm_w, w, wdt, conv_w8, conv_b, dtb)


def _proj_qkv_kernel(x_ref, nw_ref, w_ref, cos_ref, sin_ref, o0_ref, o1_ref, o2_ref, h_ref, r_ref):
    tm = x_ref.shape[0]
    j = pl.program_id(1)

    @pl.when(j == 0)
    def _():
        h_ref[...] = _rms_normed(x_ref[...], nw_ref[...])

    acc = jnp.dot(h_ref[...], w_ref[...], preferred_element_type=jnp.float32)
    cos = cos_ref[...]
    sin = sin_ref[...]
    n_cols = QKV_W // LANES
    for c in range(n_cols):
        t = acc[:, c * LANES:(c + 1) * LANES]
        if c < 2 * HEADS_PER_GROUP:
            t = t * cos + pltpu.roll(t, HEAD_DIM // 2, axis=1) * sin
        r_ref[c] = t
    for g, o_ref in enumerate((o0_ref, o1_ref, o2_ref)):
        dil = ATT_PATTERNS[g][1]

        @pl.when(j == g)
        def _(o_ref=o_ref, dil=dil):
            for r in range(dil):
                for c in range(n_cols):
                    which, hh = divmod(c, HEADS_PER_GROUP)
                    o_ref[0, r, which, :, hh * LANES:(hh + 1) * LANES] = _mxu(
                        r_ref[c, pl.ds(r, tm // dil, stride=dil), :])


def _proj_qkv(xf, norm_w, w, cos2, sin2, batch, seq_len):
    t_total = xf.shape[0]
    tm = 1024
    tps = seq_len // tm
    out_specs, out_shape = [], []
    for _, dil in ATT_PATTERNS:
        out_specs.append(pl.BlockSpec((1, dil, 3, tm // dil, ATT_OUT),
                                      lambda i, j: (i // tps, 0, 0, i % tps, 0)))
        out_shape.append(jax.ShapeDtypeStruct((batch, dil, 3, seq_len // dil, ATT_OUT), MXU_DTYPE))
    return pl.pallas_call(
        _proj_qkv_kernel,
        grid=(t_total // tm, ATT_GROUPS),
        in_specs=[
            pl.BlockSpec((tm, D_MODEL), lambda i, j: (i, 0)),
            _const_spec(norm_w, 2),
            pl.BlockSpec((D_MODEL, QKV_W), lambda i, j: (0, j)),
            pl.BlockSpec((tm, HEAD_DIM), lambda i, j: (i % tps, 0)),
            pl.BlockSpec((tm, HEAD_DIM), lambda i, j: (i % tps, 0)),
        ],
        out_specs=out_specs,
        out_shape=out_shape,
        scratch_shapes=[pltpu.VMEM((tm, D_MODEL), MXU_DTYPE),
                        pltpu.VMEM((QKV_W // LANES, tm, LANES), jnp.float32)],
        compiler_params=pltpu.CompilerParams(
            dimension_semantics=("parallel", "arbitrary"), vmem_limit_bytes=VMEM_LIMIT),
        name="proj_qkv",
    )(xf, norm_w, w, cos2, sin2)


def _attn_kernel(q_ref, kc_ref, kp_ref, kn_ref, vc_ref, vp_ref, vn_ref, o_ref, lse_ref,
                 kbuf, vbuf, *, n, tq, radius):
    i = pl.program_id(1)
    for buf, p_ref, c_ref, n_ref in ((kbuf, kp_ref, kc_ref, kn_ref), (vbuf, vp_ref, vc_ref, vn_ref)):
        buf[0:radius, :] = p_ref[0, 0]
        buf[radius:radius + tq, :] = c_ref[0, 0]
        buf[radius + tq:, :] = n_ref[0, 0]
    sub = ATT_SUB
    win = sub + 2 * radius
    row = lax.broadcasted_iota(jnp.int32, (sub, win), 0)
    col = lax.broadcasted_iota(jnp.int32, (sub, win), 1)
    in_band = jnp.abs(col - row - radius) <= radius
    lane = lax.broadcasted_iota(jnp.int32, (sub, LANES), 1)

    def body(t, carry):
        r0 = pl.multiple_of(t * sub, sub)
        kpos = i * tq + r0 - radius + col
        in_seq = kpos.astype(jnp.uint32) < jnp.uint32(n)
        lse_tile = jnp.zeros((sub, LANES), jnp.float32)
        for h in range(HEADS_PER_GROUP):
            hs = slice(h * HEAD_DIM, (h + 1) * HEAD_DIM)
            q = q_ref[0, 0, pl.ds(r0, sub), hs]
            kw = kbuf[pl.ds(r0, win), hs]
            vw = vbuf[pl.ds(r0, win), hs]
            s = lax.dot_general(q, kw, (((1,), (1,)), ((), ())), preferred_element_type=jnp.float32)
            s = s * (1.0 / math.sqrt(HEAD_DIM))
            s = jnp.where(in_band, jnp.where(in_seq, s, -jnp.inf), -jnp.inf)
            m = jnp.max(s, axis=-1, keepdims=True)
            e = jnp.exp(s - m)
            den = jnp.sum(e, axis=-1, keepdims=True)
            o = jnp.dot(_mxu(e), vw, preferred_element_type=jnp.float32) / den
            o_ref[0, pl.ds(r0, sub), hs] = o.astype(o_ref.dtype)
            lse_tile = jnp.where(lane == h, m + jnp.log(den), lse_tile)
        lse_ref[0, pl.ds(r0, sub), :] = lse_tile
        return carry

    lax.fori_loop(0, tq // sub, body, 0)


def _attn_group(qkv, radius):
    s_total, _, n, _ = qkv.shape
    tq = min(1024, n)
    per = tq // radius
    last = n // radius - 1
    kern = functools.partial(_attn_kernel, n=n, tq=tq, radius=radius)
    cur = lambda which: pl.BlockSpec((1, 1, tq, ATT_OUT), lambda s, i: (s, which, i, 0))
    prev = lambda which: pl.BlockSpec(
        (1, 1, radius, ATT_OUT), lambda s, i: (s, which, jnp.maximum(i * per - 1, 0), 0))
    nxt = lambda which: pl.BlockSpec(
        (1, 1, radius, ATT_OUT), lambda s, i: (s, which, jnp.minimum((i + 1) * per, last), 0))
    return pl.pallas_call(
        kern,
        grid=(s_total, n // tq),
        in_specs=[cur(0), cur(1), prev(1), nxt(1), cur(2), prev(2), nxt(2)],
        out_specs=[
            pl.BlockSpec((1, tq, ATT_OUT), lambda s, i: (s, i, 0)),
            pl.BlockSpec((1, tq, LANES), lambda s, i: (s, i, 0)),
        ],
        out_shape=[
            jax.ShapeDtypeStruct((s_total, n, ATT_OUT), MXU_DTYPE),
            jax.ShapeDtypeStruct((s_total, n, LANES), jnp.float32),
        ],
        scratch_shapes=[pltpu.VMEM((tq + 2 * radius, ATT_OUT), MXU_DTYPE),
                        pltpu.VMEM((tq + 2 * radius, ATT_OUT), MXU_DTYPE)],
        compiler_params=pltpu.CompilerParams(
            dimension_semantics=("parallel", "arbitrary"), vmem_limit_bytes=VMEM_LIMIT),
        name="band_attn",
    )(qkv, qkv, qkv, qkv, qkv, qkv, qkv)


def _ssd_kernel(*refs, rev):
    if rev:
        xs_ref, bc_ref, dt_ref, sz_ref, yf_ref, alog_ref, e_ref, nw_ref, o_ref, st_ref = refs
    else:
        xs_ref, bc_ref, dt_ref, alog_ref, e_ref, dskip_ref, o_ref, st_ref = refs
    q = SSD_CHUNK

    @pl.when(pl.program_id(1) == 0)
    def _():
        st_ref[...] = jnp.zeros_like(st_ref)

    dt = dt_ref[...]
    adt = dt * (-jnp.exp(alog_ref[...]))
    row = lax.broadcasted_iota(jnp.int32, (q, q), 0)
    col = lax.broadcasted_iota(jnp.int32, (q, q), 1)
    inside = (col >= row) if rev else (col <= row)
    acum = _split_dot(jnp.where(inside, 1.0, 0.0).astype(MXU_DTYPE), adt)
    tot = acum[0:1, :] if rev else acum[q - 1:q, :]
    acum_t = acum.T
    dt_t = dt.T
    e_mat = e_ref[...]
    ex_acc = _split_dot_r(jnp.exp(acum), e_mat)
    ex_w = _split_dot_r(jnp.exp(tot - acum) * dt, e_mat)
    dec_row = ex_acc[0:1, :] if rev else ex_acc[q - 1:q, :]
    lane = lax.broadcasted_iota(jnp.int32, (q, LANES), 1)
    lane_off = SSM_HEADS if rev else 0

    for g in range(SSM_GROUPS):
        gsl = slice(g * GROUP_W, (g + 1) * GROUP_W)
        b_g = bc_ref[:, g * D_STATE:(g + 1) * D_STATE]
        c_g = bc_ref[:, GROUP_W + g * D_STATE:GROUP_W + (g + 1) * D_STATE]
        cb = lax.dot_general(c_g, b_g, (((1,), (1,)), ((), ())), preferred_element_type=jnp.float32)
        x_g = xs_ref[:, gsl]
        y_g = jnp.dot(c_g, _mxu(st_ref[:, gsl]), preferred_element_type=jnp.float32) * ex_acc[:, gsl]
        parts = []
        for jj in range(HEADS_PER_SSM_GROUP // 2):
            ms = []
            for h in (g * HEADS_PER_SSM_GROUP + 2 * jj, g * HEADS_PER_SSM_GROUP + 2 * jj + 1):
                hl = lane_off + h
                seg = acum[:, hl:hl + 1] - acum_t[hl:hl + 1, :]
                lm = jnp.exp(jnp.where(inside, seg, -jnp.inf))
                ms.append(cb * lm * dt_t[hl:hl + 1, :])
            m_pair = _mxu(jnp.concatenate(ms, axis=1))
            xp = x_g[:, jj * LANES:(jj + 1) * LANES]
            r_pair = _mxu(jnp.concatenate(
                [jnp.where(lane < SSM_HEAD_DIM, xp, 0.0), jnp.where(lane >= SSM_HEAD_DIM, xp, 0.0)],
                axis=0))
            parts.append(jnp.dot(m_pair, r_pair, preferred_element_type=jnp.float32))
        y_g = y_g + jnp.concatenate(parts, axis=1)
        b_t = _mxu(b_g.astype(jnp.float32).T)
        s_new = jnp.dot(b_t, _mxu(x_g * ex_w[:, gsl]), preferred_element_type=jnp.float32)
        st_ref[:, gsl] = st_ref[:, gsl] * dec_row[:, gsl] + s_new
        if rev:
            y = (yf_ref[:, gsl] + y_g) * sz_ref[:, gsl].astype(jnp.float32)
            ms_ = jnp.mean(y * y, axis=-1, keepdims=True)
            o_ref[:, gsl] = (y * lax.rsqrt(ms_ + NORM_EPS) * nw_ref[:, gsl]).astype(o_ref.dtype)
        else:
            o_ref[:, gsl] = y_g + x_g * dskip_ref[:, gsl]


def _ssd_sweep(xs, bc, dt, zg, yf, alog, e_mat, row_w, batch, seq_len, rev):
    t_total = batch * seq_len
    q = SSD_CHUNK
    nc = seq_len // q
    rowmap = (lambda b, c: b * nc + nc - 1 - c) if rev else (lambda b, c: b * nc + c)
    row = lambda width: pl.BlockSpec((q, width), lambda b, c: (rowmap(b, c), 0))
    specs = [row(D_INNER), row(BC_W), row(LANES)]
    args = [xs, bc, dt]
    if rev:
        specs += [row(D_INNER), row(D_INNER)]
        args += [zg, yf]
    specs += [_const_spec(alog, 2), _const_spec(e_mat, 2), _const_spec(row_w, 2)]
    args += [alog, e_mat, row_w]
    return pl.pallas_call(
        functools.partial(_ssd_kernel, rev=rev),
        grid=(batch, nc),
        in_specs=specs,
        out_specs=row(D_INNER),
        out_shape=jax.ShapeDtypeStruct((t_total, D_INNER), MXU_DTYPE if rev else jnp.float32),
        scratch_shapes=[pltpu.VMEM((D_STATE, D_INNER), jnp.float32)],
        compiler_params=pltpu.CompilerParams(
            dimension_semantics=("parallel", "arbitrary"), vmem_limit_bytes=VMEM_LIMIT),
        name="ssd_bwd" if rev else "ssd_fwd",
    )(*args)


def _mix_kernel(x_ref, a0_ref, a1_ref, a2_ref, l0_ref, l1_ref, l2_ref, ssm_ref, sg_ref,
                wao_ref, wso_ref, wout_ref, o_ref, as_ref, ls_ref):
    tm = x_ref.shape[0]
    for gi, (a_ref, l_ref) in enumerate(((a1_ref, l1_ref), (a2_ref, l2_ref))):
        dil = ATT_PATTERNS[gi + 1][1]
        for r in range(dil):
            rows = pl.ds(r, tm // dil, stride=dil)
            a = a_ref[0, r].astype(jnp.float32)
            for h in range(HEADS_PER_GROUP):
                as_ref[gi, h, rows, :] = a[:, h * HEAD_DIM:(h + 1) * HEAD_DIM]
            ls_ref[gi, rows, :] = l_ref[0, r]
    l0, l1, l2 = l0_ref[0, 0], ls_ref[0], ls_ref[1]
    m = jnp.maximum(jnp.maximum(l0, l1), l2)
    e0, e1, e2 = jnp.exp(l0 - m), jnp.exp(l1 - m), jnp.exp(l2 - m)
    den = e0 + e1 + e2
    al0, al1, al2 = e0 / den, e1 / den, e2 / den
    heads = []
    for h in range(HEADS_PER_GROUP):
        hs = slice(h * HEAD_DIM, (h + 1) * HEAD_DIM)
        heads.append(al0[:, h:h + 1] * a0_ref[0, 0, :, hs].astype(jnp.float32)
                     + al1[:, h:h + 1] * as_ref[0, h] + al2[:, h:h + 1] * as_ref[1, h])
    att = jnp.concatenate(heads, axis=1)
    pa = jnp.dot(_mxu(att), wao_ref[...], preferred_element_type=jnp.float32)
    ps = jnp.dot(ssm_ref[...], wso_ref[...], preferred_element_type=jnp.float32)
    mixed = (sg_ref[:, :D_MODEL].astype(jnp.float32) * pa
             + sg_ref[:, D_MODEL:].astype(jnp.float32) * ps)
    o_ref[...] = x_ref[...] + jnp.dot(_mxu(mixed), wout_ref[...], preferred_element_type=jnp.float32)


def _mix(xf, att, lse, ssm, zg, wao, wso, wout, batch, seq_len):
    t_total = xf.shape[0]
    tm = 512
    tps = seq_len // tm
    row = lambda w, blk=0: pl.BlockSpec((tm, w), lambda i: (i, blk))
    cls = lambda dil, w: pl.BlockSpec((1, dil, tm // dil, w), lambda i: (i // tps, 0, i % tps, 0))
    dils = [d for _, d in ATT_PATTERNS]
    att4 = [a.reshape(batch, d, seq_len // d, ATT_OUT) for a, d in zip(att, dils)]
    lse4 = [l.reshape(batch, d, seq_len // d, LANES) for l, d in zip(lse, dils)]
    return pl.pallas_call(
        _mix_kernel,
        grid=(t_total // tm,),
        in_specs=[row(D_MODEL)] + [cls(d, ATT_OUT) for d in dils] + [cls(d, LANES) for d in dils]
        + [row(D_INNER), row(D_INNER, 1), _const_spec(wao, 1), _const_spec(wso, 1),
           _const_spec(wout, 1)],
        out_specs=row(D_MODEL),
        out_shape=jax.ShapeDtypeStruct((t_total, D_MODEL), jnp.float32),
        scratch_shapes=[pltpu.VMEM((ATT_GROUPS - 1, HEADS_PER_GROUP, tm, HEAD_DIM), jnp.float32),
                        pltpu.VMEM((ATT_GROUPS - 1, tm, LANES), jnp.float32)],
        compiler_params=pltpu.CompilerParams(
            dimension_semantics=("parallel",), vmem_limit_bytes=VMEM_LIMIT),
        name="mix_out",
    )(xf, *att4, *lse4, ssm, zg, wao, wso, wout)


FFN_CHUNK = 1408


def _ffn_kernel(x_ref, nw_ref, wgu_ref, wd_ref, fw_ref, o_ref, *, final_norm):
    x = x_ref[...]
    h = _rms_normed(x, nw_ref[...])
    acc = x
    for c0 in range(0, D_FF, FFN_CHUNK):
        gate = jnp.dot(h, wgu_ref[:, c0:c0 + FFN_CHUNK], preferred_element_type=jnp.float32)
        up = jnp.dot(h, wgu_ref[:, D_FF + c0:D_FF + c0 + FFN_CHUNK],
                     preferred_element_type=jnp.float32)
        a = _mxu(gate * _sigmoid(gate) * up)
        acc = acc + jnp.dot(a, wd_ref[c0:c0 + FFN_CHUNK, :], preferred_element_type=jnp.float32)
    if final_norm:
        ms2 = jnp.mean(acc * acc, axis=-1, keepdims=True)
        acc = acc * lax.rsqrt(ms2 + NORM_EPS) * fw_ref[...]
    o_ref[...] = acc


def _ffn(x1, norm_w, wgu, wd, final_w, final_norm):
    t_total = x1.shape[0]
    tm = 512
    row = pl.BlockSpec((tm, D_MODEL), lambda i: (i, 0))
    return pl.pallas_call(
        functools.partial(_ffn_kernel, final_norm=final_norm),
        grid=(t_total // tm,),
        in_specs=[row, _const_spec(norm_w, 1), _const_spec(wgu, 1), _const_spec(wd, 1),
                  _const_spec(final_w, 1)],
        out_specs=row,
        out_shape=jax.ShapeDtypeStruct((t_total, D_MODEL), jnp.float32),
        compiler_params=pltpu.CompilerParams(
            dimension_semantics=("parallel",), vmem_limit_bytes=VMEM_LIMIT),
        name="ffn",
    )(x1, norm_w, wgu, wd, final_w)


def _pad_lanes(a, width):
    return jnp.pad(a, [(0, 0)] * (a.ndim - 1) + [(0, width - a.shape[-1])])


def _rope_tables(seq_len):
    half = HEAD_DIM // 2
    inv = ROPE_THETA ** (-jnp.arange(half, dtype=jnp.float32) / half)
    ang = jnp.arange(seq_len).astype(jnp.float32)[:, None] * inv[None, :]
    cos, sin = jnp.cos(ang), jnp.sin(ang)
    return jnp.concatenate([cos, cos], axis=1), jnp.concatenate([-sin, sin], axis=1)


def _expand_matrix(row0):
    r = np.zeros((LANES, D_INNER), np.float32)
    for h in range(SSM_HEADS):
        r[row0 + h, h * SSM_HEAD_DIM:(h + 1) * SSM_HEAD_DIM] = 1.0
    return jnp.asarray(r, MXU_DTYPE)


def _prep_params(norm_mix, w_in, conv_w, conv_b, a_log, dt_bias, d_skip, ssm_norm,
                 w_attn_out, w_ssm_out, w_out, norm_ffn, w_gate_up, w_down, norm_final):
    b = [int(v) for v in np.cumsum([ATT_W, ATT_W, ATT_W, D_INNER, CONV_CH, 2 * SSM_HEADS, D_MODEL])]
    q, k, v = w_in[..., :b[0]], w_in[..., b[0]:b[1]], w_in[..., b[1]:b[2]]
    z, xbc, dt = w_in[..., b[2]:b[3]], w_in[..., b[3]:b[4]], w_in[..., b[4]:b[5]]
    ga, gm = w_in[..., b[5]:b[6]], w_in[..., b[6]:]
    grp = lambda t, g: t[..., g * ATT_OUT:(g + 1) * ATT_OUT]
    w_qkv = jnp.concatenate([grp(t, g) for g in range(ATT_GROUPS) for t in (q, k, v)], axis=-1)
    return {
        "norm_mix": norm_mix[:, None, :],
        "w_gates": _mxu(jnp.concatenate([z, ga, gm], axis=-1)),
        "w_xbc": _mxu(xbc),
        "w_dt": _mxu(_pad_lanes(dt, LANES)),
        "w_qkv": _mxu(w_qkv),
        "conv_w": jnp.pad(conv_w, ((0, 0), (0, SUBLANES - CONV_K), (0, 0))),
        "conv_b": conv_b[:, None, :],
        "a_log": _pad_lanes(a_log.reshape(DEPTH, 1, 2 * SSM_HEADS), LANES),
        "dt_bias": _pad_lanes(dt_bias.reshape(DEPTH, 1, 2 * SSM_HEADS), LANES),
        "d_skip": jnp.repeat(d_skip, SSM_HEAD_DIM, axis=-1)[:, None, :],
        "ssm_norm": ssm_norm[:, None, :],
        "w_attn_out": _mxu(w_attn_out),
        "w_ssm_out": _mxu(w_ssm_out),
        "w_out": _mxu(w_out),
        "norm_ffn": norm_ffn[:, None, :],
        "w_gate_up": _mxu(w_gate_up),
        "w_down": _mxu(w_down),
        "norm_final": norm_final[None, :],
        "e_fwd": _expand_matrix(0),
        "e_bwd": _expand_matrix(SSM_HEADS),
    }


def _trunk(x, p):
    batch, seq_len, _ = x.shape
    xf = x.reshape(batch * seq_len, D_MODEL)
    cos2, sin2 = _rope_tables(seq_len)
    for i in range(DEPTH):
        nw = p["norm_mix"][i]
        zg = _proj_gates(xf, nw, p["w_gates"][i])
        xs, bc, dt = _proj_xbc(xf, nw, p["w_xbc"][i], p["w_dt"][i], p["conv_w"][i], p["conv_b"][i],
                               p["dt_bias"][i], seq_len)
        qkv = _proj_qkv(xf, nw, p["w_qkv"][i], cos2, sin2, batch, seq_len)
        att, lse = [], []
        for g, (window, dil) in enumerate(ATT_PATTERNS):
            n = seq_len // dil
            o, l = _attn_group(qkv[g].reshape(batch * dil, 3, n, ATT_OUT), window // (2 * dil))
            att.append(o)
            lse.append(l)
        yf = _ssd_sweep(xs, bc, dt, None, None, p["a_log"][i], p["e_fwd"], p["d_skip"][i],
                        batch, seq_len, False)
        ssm = _ssd_sweep(xs, bc, dt, zg, yf, p["a_log"][i], p["e_bwd"], p["ssm_norm"][i],
                         batch, seq_len, True)
        x1 = _mix(xf, att, lse, ssm, zg, p["w_attn_out"][i], p["w_ssm_out"][i], p["w_out"][i],
                  batch, seq_len)
        xf = _ffn(x1, p["norm_ffn"][i], p["w_gate_up"][i], p["w_down"][i], p["norm_final"],
                  i == DEPTH - 1)
    return xf.reshape(batch, seq_len, D_MODEL)


def kernel(x_prompt, x_sample, norm_mix, w_in, conv_w, conv_b, a_log, dt_bias, d_skip, ssm_norm,
           w_attn_out, w_ssm_out, w_out, norm_ffn, w_gate_up, w_down, norm_final):
    p = _prep_params(norm_mix, w_in, conv_w, conv_b, a_log, dt_bias, d_skip, ssm_norm,
                     w_attn_out, w_ssm_out, w_out, norm_ffn, w_gate_up, w_down, norm_final)
    return (_trunk(x_prompt, p), _trunk(x_sample, p))
```

```python
import functools
import math

import jax
import jax.numpy as jnp
import numpy as np
from jax import lax
from jax.experimental import pallas as pl
from jax.experimental.pallas import tpu as pltpu

D_MODEL = 1024
DEPTH = 4
HEAD_DIM = 128
ATT_PATTERNS = ((128, 1), (512, 4), (2048, 16))
HEADS_PER_GROUP = 4
ATT_GROUPS = len(ATT_PATTERNS)
ATT_W = HEADS_PER_GROUP * ATT_GROUPS * HEAD_DIM
ATT_OUT = HEADS_PER_GROUP * HEAD_DIM
ROPE_THETA = 10000.0
D_INNER = 2048
SSM_HEAD_DIM = 64
SSM_HEADS = D_INNER // SSM_HEAD_DIM
SSM_GROUPS = 4
D_STATE = 128
CONV_K = 5
CONV_PAD = CONV_K // 2
BC_W = 2 * SSM_GROUPS * D_STATE
CONV_CH = D_INNER + BC_W
D_FF = 2816
NORM_EPS = 1e-6

MXU_DTYPE = jnp.bfloat16
LANES = 128
SUBLANES = 8
PACKED_ROWS = 16
SSD_CHUNK = 128
SSD_STEP = 256
LOG2_E = 1.4426950408889634
GROUP_W = D_INNER // SSM_GROUPS
HEADS_PER_SSM_GROUP = SSM_HEADS // SSM_GROUPS
ATT_SUB = 128
QKV_W = 3 * ATT_OUT

VMEM_LIMIT = 56 * 1024 * 1024


def _mxu(x):
    return x.astype(MXU_DTYPE)


def _sigmoid(x):
    return 1.0 / (1.0 + jnp.exp(-x))


def _split3(x):
    hi = _mxu(x)
    r1 = x - hi.astype(jnp.float32)
    mid = _mxu(r1)
    return hi, mid, _mxu(r1 - mid.astype(jnp.float32))


def _split_dot(lhs01, x):
    return sum(jnp.dot(lhs01, p, preferred_element_type=jnp.float32) for p in _split3(x))


def _split_dot_r(x, rhs01):
    return sum(jnp.dot(p, rhs01, preferred_element_type=jnp.float32) for p in _split3(x))


def _rms_normed(x, w):
    ms = jnp.mean(x * x, axis=-1, keepdims=True)
    return _mxu(x * lax.rsqrt(ms + NORM_EPS) * w)


def _const_spec(a, grid_rank):
    zeros = (0,) * a.ndim
    if grid_rank == 1:
        return pl.BlockSpec(a.shape, lambda i: zeros, pipeline_mode=pl.Buffered(1))
    return pl.BlockSpec(a.shape, lambda i, j: zeros, pipeline_mode=pl.Buffered(1))


def _proj_gates_kernel(x_ref, nw_ref, w_ref, o_ref, h_ref):
    j = pl.program_id(1)

    @pl.when(j == 0)
    def _():
        h_ref[...] = _rms_normed(x_ref[...], nw_ref[...])

    acc = jnp.dot(h_ref[...], w_ref[...], preferred_element_type=jnp.float32)
    s = _sigmoid(acc)
    o_ref[...] = jnp.where(j == 0, acc * s, s).astype(o_ref.dtype)


def _proj_gates(xf, norm_w, w):
    t_total = xf.shape[0]
    tm, tn = 1024, D_INNER
    return pl.pallas_call(
        _proj_gates_kernel,
        grid=(t_total // tm, w.shape[1] // tn),
        in_specs=[
            pl.BlockSpec((tm, D_MODEL), lambda i, j: (i, 0)),
            _const_spec(norm_w, 2),
            pl.BlockSpec((D_MODEL, tn), lambda i, j: (0, j)),
        ],
        out_specs=pl.BlockSpec((tm, tn), lambda i, j: (i, j)),
        out_shape=jax.ShapeDtypeStruct((t_total, w.shape[1]), MXU_DTYPE),
        scratch_shapes=[pltpu.VMEM((tm, D_MODEL), MXU_DTYPE)],
        compiler_params=pltpu.CompilerParams(
            dimension_semantics=("parallel", "arbitrary"), vmem_limit_bytes=VMEM_LIMIT),
        name="proj_gates",
    )(xf, norm_w, w)


def _proj_xbc_kernel(x_ref, xp_ref, xn_ref, nw_ref, w_ref, wdt_ref, cw_ref, cbias_ref, dtb_ref,
                     xs_ref, bc_ref, dt_ref, h_ref, acc_ref, *, tiles_per_seq):
    tm = x_ref.shape[0]
    hr = PACKED_ROWS
    iseq = pl.program_id(0) % tiles_per_seq
    nw = nw_ref[...]
    h_ref[0:hr, :] = _rms_normed(xp_ref[...], nw)
    h_ref[hr:hr + tm, :] = _rms_normed(x_ref[...], nw)
    h_ref[hr + tm:, :] = _rms_normed(xn_ref[...], nw)
    acc_ref[...] = jnp.dot(h_ref[...], w_ref[...], preferred_element_type=jnp.float32)
    acc_ref[0:hr, :] = jnp.where(iseq > 0, acc_ref[0:hr, :], 0.0)
    acc_ref[hr + tm:, :] = jnp.where(iseq < tiles_per_seq - 1, acc_ref[hr + tm:, :], 0.0)
    for cc in range(CONV_CH // GROUP_W):
        sl = slice(cc * GROUP_W, (cc + 1) * GROUP_W)
        full = acc_ref[:, sl]
        a = cbias_ref[:, sl] + cw_ref[CONV_PAD:CONV_PAD + 1, sl] * full[hr:hr + tm]
        for k in range(CONV_K):
            if k != CONV_PAD:
                shifted = pltpu.roll(full, (CONV_PAD - k) % (tm + 2 * hr), axis=0)
                a = a + cw_ref[k:k + 1, sl] * shifted[hr:hr + tm]
        y = a * _sigmoid(a)
        if cc < D_INNER // GROUP_W:
            xs_ref[:, sl] = y
        else:
            bc_ref[:, cc * GROUP_W - D_INNER:(cc + 1) * GROUP_W - D_INNER] = _mxu(y)
    v = jnp.dot(h_ref[hr:hr + tm, :], wdt_ref[...], preferred_element_type=jnp.float32) + dtb_ref[...]
    dt_ref[...] = jnp.maximum(v, 0.0) + jnp.log1p(jnp.exp(-jnp.abs(v)))


def _proj_xbc(xf, norm_w, w, wdt, conv_w8, conv_b, dtb, seq_len):
    t_total = xf.shape[0]
    tm = 512
    hr = PACKED_ROWS
    per = tm // hr
    last = t_total // hr - 1
    row = lambda width: pl.BlockSpec((tm, width), lambda i: (i, 0))
    kern = functools.partial(_proj_xbc_kernel, tiles_per_seq=seq_len // tm)
    return pl.pallas_call(
        kern,
        grid=(t_total // tm,),
        in_specs=[
            row(D_MODEL),
            pl.BlockSpec((hr, D_MODEL), lambda i: (jnp.maximum(i * per - 1, 0), 0)),
            pl.BlockSpec((hr, D_MODEL), lambda i: (jnp.minimum((i + 1) * per, last), 0)),
            _const_spec(norm_w, 1), _const_spec(w, 1), _const_spec(wdt, 1),
            _const_spec(conv_w8, 1), _const_spec(conv_b, 1), _const_spec(dtb, 1),
        ],
        out_specs=[row(D_INNER), row(BC_W), row(LANES)],
        out_shape=[
            jax.ShapeDtypeStruct((t_total, D_INNER), jnp.float32),
            jax.ShapeDtypeStruct((t_total, BC_W), MXU_DTYPE),
            jax.ShapeDtypeStruct((t_total, LANES), jnp.float32),
        ],
        scratch_shapes=[
            pltpu.VMEM((tm + 2 * hr, D_MODEL), MXU_DTYPE),
            pltpu.VMEM((tm + 2 * hr, CONV_CH), jnp.float32),
        ],
        compiler_params=pltpu.CompilerParams(
            dimension_semantics=("parallel",), vmem_limit_bytes=VMEM_LIMIT),
        name="proj_xbc",
    )(xf, xf, xf, norm_w, w, wdt, conv_w8, conv_b, dtb)


def _proj_qkv_kernel(x_ref, nw_ref, w_ref, cos_ref, sin_ref, o0_ref, o1_ref, o2_ref, h_ref, r_ref):
    tm = x_ref.shape[0]
    j = pl.program_id(1)

    @pl.when(j == 0)
    def _():
        h_ref[...] = _rms_normed(x_ref[...], nw_ref[...])

    acc = jnp.dot(h_ref[...], w_ref[...], preferred_element_type=jnp.float32)
    cos = cos_ref[...]
    sin = sin_ref[...]
    n_cols = QKV_W // LANES
    for c in range(n_cols):
        t = acc[:, c * LANES:(c + 1) * LANES]
        if c < 2 * HEADS_PER_GROUP:
            t = t * cos + pltpu.roll(t, HEAD_DIM // 2, axis=1) * sin
        r_ref[c] = t
    for g, o_ref in enumerate((o0_ref, o1_ref, o2_ref)):
        dil = ATT_PATTERNS[g][1]

        @pl.when(j == g)
        def _(o_ref=o_ref, dil=dil):
            for r in range(dil):
                for c in range(n_cols):
                    which, hh = divmod(c, HEADS_PER_GROUP)
                    o_ref[0, r, which, :, hh * LANES:(hh + 1) * LANES] = _mxu(
                        r_ref[c, pl.ds(r, tm // dil, stride=dil), :])


def _proj_qkv(xf, norm_w, w, cos2, sin2, batch, seq_len):
    t_total = xf.shape[0]
    tm = 1024
    tps = seq_len // tm
    out_specs, out_shape = [], []
    for _, dil in ATT_PATTERNS:
        out_specs.append(pl.BlockSpec((1, dil, 3, tm // dil, ATT_OUT),
                                      lambda i, j: (i // tps, 0, 0, i % tps, 0)))
        out_shape.append(jax.ShapeDtypeStruct((batch, dil, 3, seq_len // dil, ATT_OUT), MXU_DTYPE))
    return pl.pallas_call(
        _proj_qkv_kernel,
        grid=(t_total // tm, ATT_GROUPS),
        in_specs=[
            pl.BlockSpec((tm, D_MODEL), lambda i, j: (i, 0)),
            _const_spec(norm_w, 2),
            pl.BlockSpec((D_MODEL, QKV_W), lambda i, j: (0, j)),
            pl.BlockSpec((tm, HEAD_DIM), lambda i, j: (i % tps, 0)),
            pl.BlockSpec((tm, HEAD_DIM), lambda i, j: (i % tps, 0)),
        ],
        out_specs=out_specs,
        out_shape=out_shape,
        scratch_shapes=[pltpu.VMEM((tm, D_MODEL), MXU_DTYPE),
                        pltpu.VMEM((QKV_W // LANES, tm, LANES), jnp.float32)],
        compiler_params=pltpu.CompilerParams(
            dimension_semantics=("parallel", "arbitrary"), vmem_limit_bytes=VMEM_LIMIT),
        name="proj_qkv",
    )(xf, norm_w, w, cos2, sin2)


def _attn_kernel(q_ref, kc_ref, kp_ref, kn_ref, vc_ref, vp_ref, vn_ref, o_ref, lse_ref,
                 kbuf, vbuf, *, n, tq, radius):
    i = pl.program_id(1)
    for buf, p_ref, c_ref, n_ref in ((kbuf, kp_ref, kc_ref, kn_ref), (vbuf, vp_ref, vc_ref, vn_ref)):
        buf[0:radius, :] = p_ref[0, 0]
        buf[radius:radius + tq, :] = c_ref[0, 0]
        buf[radius + tq:, :] = n_ref[0, 0]
    sub = ATT_SUB
    win = sub + 2 * radius
    row = lax.broadcasted_iota(jnp.int32, (sub, win), 0)
    col = lax.broadcasted_iota(jnp.int32, (sub, win), 1)
    in_band = jnp.abs(col - row - radius) <= radius
    lane = lax.broadcasted_iota(jnp.int32, (sub, LANES), 1)

    def body(t, carry):
        r0 = pl.multiple_of(t * sub, sub)
        kpos = i * tq + r0 - radius + col
        in_seq = kpos.astype(jnp.uint32) < jnp.uint32(n)
        bias = jnp.where(in_band, jnp.where(in_seq, 0.0, -jnp.inf), -jnp.inf)
        lse_tile = jnp.zeros((sub, LANES), jnp.float32)
        for h in range(HEADS_PER_GROUP):
            hs = slice(h * HEAD_DIM, (h + 1) * HEAD_DIM)
            q = q_ref[0, 0, pl.ds(r0, sub), hs]
            kw = kbuf[pl.ds(r0, win), hs]
            vw = vbuf[pl.ds(r0, win), hs]
            s = lax.dot_general(q, kw, (((1,), (1,)), ((), ())), preferred_element_type=jnp.float32)
            s = s * (1.0 / math.sqrt(HEAD_DIM)) + bias
            m = jnp.max(s, axis=-1, keepdims=True)
            e = jnp.exp(s - m)
            den = jnp.sum(e, axis=-1, keepdims=True)
            o = jnp.dot(_mxu(e), vw, preferred_element_type=jnp.float32) / den
            o_ref[0, pl.ds(r0, sub), hs] = o.astype(o_ref.dtype)
            lse_tile = jnp.where(lane == h, m + jnp.log(den), lse_tile)
        lse_ref[0, pl.ds(r0, sub), :] = lse_tile
        return carry

    lax.fori_loop(0, tq // sub, body, 0, unroll=min(4, tq // sub))


def _attn_group(qkv, radius):
    s_total, _, n, _ = qkv.shape
    tq = min(1024, n)
    per = tq // radius
    last = n // radius - 1
    kern = functools.partial(_attn_kernel, n=n, tq=tq, radius=radius)
    cur = lambda which: pl.BlockSpec((1, 1, tq, ATT_OUT), lambda s, i: (s, which, i, 0))
    prev = lambda which: pl.BlockSpec(
        (1, 1, radius, ATT_OUT), lambda s, i: (s, which, jnp.maximum(i * per - 1, 0), 0))
    nxt = lambda which: pl.BlockSpec(
        (1, 1, radius, ATT_OUT), lambda s, i: (s, which, jnp.minimum((i + 1) * per, last), 0))
    return pl.pallas_call(
        kern,
        grid=(s_total, n // tq),
        in_specs=[cur(0), cur(1), prev(1), nxt(1), cur(2), prev(2), nxt(2)],
        out_specs=[
            pl.BlockSpec((1, tq, ATT_OUT), lambda s, i: (s, i, 0)),
            pl.BlockSpec((1, tq, LANES), lambda s, i: (s, i, 0)),
        ],
        out_shape=[
            jax.ShapeDtypeStruct((s_total, n, ATT_OUT), MXU_DTYPE),
            jax.ShapeDtypeStruct((s_total, n, LANES), jnp.float32),
        ],
        scratch_shapes=[pltpu.VMEM((tq + 2 * radius, ATT_OUT), MXU_DTYPE),
                        pltpu.VMEM((tq + 2 * radius, ATT_OUT), MXU_DTYPE)],
        compiler_params=pltpu.CompilerParams(
            dimension_semantics=("parallel", "arbitrary"), vmem_limit_bytes=VMEM_LIMIT),
        name="band_attn",
    )(qkv, qkv, qkv, qkv, qkv, qkv, qkv)


def _ssd_kernel(*refs, rev):
    if rev:
        xs_ref, bc_ref, dt_ref, sz_ref, yf_ref, alog_ref, e_ref, nw_ref, o_ref, st_ref = refs
    else:
        xs_ref, bc_ref, dt_ref, alog_ref, e_ref, dskip_ref, o_ref, st_ref = refs
    q = SSD_CHUNK

    @pl.when(pl.program_id(1) == 0)
    def _():
        st_ref[...] = jnp.zeros_like(st_ref)

    row = lax.broadcasted_iota(jnp.int32, (q, q), 0)
    col = lax.broadcasted_iota(jnp.int32, (q, q), 1)
    inside = (col >= row) if rev else (col <= row)
    tri = jnp.where(inside, 1.0, 0.0).astype(MXU_DTYPE)
    lane = lax.broadcasted_iota(jnp.int32, (q, LANES), 1)
    lane_off = SSM_HEADS if rev else 0
    e_mat = e_ref[...]
    neg_a = jnp.exp(alog_ref[...])
    expand = lambda f: jnp.dot(_mxu(f), e_mat, preferred_element_type=jnp.float32)
    chunks = range(SSD_STEP // q)
    gsls = [slice(g * GROUP_W, (g + 1) * GROUP_W) for g in range(SSM_GROUPS)]

    for ci in (reversed(chunks) if rev else chunks):
        rows = slice(ci * q, (ci + 1) * q)
        b_gs = [bc_ref[rows, g * D_STATE:(g + 1) * D_STATE] for g in range(SSM_GROUPS)]
        c_gs = [bc_ref[rows, GROUP_W + g * D_STATE:GROUP_W + (g + 1) * D_STATE]
                for g in range(SSM_GROUPS)]
        cbs = [jnp.where(inside, lax.dot_general(c_gs[g], b_gs[g], (((1,), (1,)), ((), ())),
                                                 preferred_element_type=jnp.float32), 0.0)
               for g in range(SSM_GROUPS)]
        y_offs = [jnp.dot(c_gs[g], _mxu(st_ref[:, gsls[g]]), preferred_element_type=jnp.float32)
                  for g in range(SSM_GROUPS)]
        dt = dt_ref[rows, :]
        acum = _split_dot(tri, dt * (-neg_a))
        tot = acum[0:1, :] if rev else acum[q - 1:q, :]
        acum2 = acum * LOG2_E
        acum2_t = acum2.T
        dt_t = dt.T
        e_acc = jnp.exp(acum)
        ex_acc = expand(e_acc)
        ex_w = expand(jnp.exp(tot - acum) * dt)
        edge = _split_dot_r(e_acc[0:SUBLANES, :] if rev else e_acc[q - SUBLANES:, :], e_mat)
        dec_row = edge[0:1, :] if rev else edge[SUBLANES - 1:, :]

        for g in range(SSM_GROUPS):
            gsl = gsls[g]
            x_g = xs_ref[rows, gsl]
            parts = []
            for jj in range(HEADS_PER_SSM_GROUP // 2):
                ms = []
                for h in (g * HEADS_PER_SSM_GROUP + 2 * jj, g * HEADS_PER_SSM_GROUP + 2 * jj + 1):
                    hl = lane_off + h
                    seg2 = jnp.minimum(acum2[:, hl:hl + 1] - acum2_t[hl:hl + 1, :], 0.0)
                    ms.append(cbs[g] * jnp.exp2(seg2) * dt_t[hl:hl + 1, :])
                m_pair = _mxu(jnp.concatenate(ms, axis=1))
                xp = x_g[:, jj * LANES:(jj + 1) * LANES]
                r_pair = _mxu(jnp.concatenate(
                    [jnp.where(lane < SSM_HEAD_DIM, xp, 0.0), jnp.where(lane >= SSM_HEAD_DIM, xp, 0.0)],
                    axis=0))
                parts.append(jnp.dot(m_pair, r_pair, preferred_element_type=jnp.float32))
            y_g = y_offs[g] * ex_acc[:, gsl] + jnp.concatenate(parts, axis=1)
            b_t = _mxu(b_gs[g].astype(jnp.float32).T)
            s_new = jnp.dot(b_t, _mxu(x_g * ex_w[:, gsl]), preferred_element_type=jnp.float32)
            st_ref[:, gsl] = st_ref[:, gsl] * dec_row[:, gsl] + s_new
            if rev:
                y = (yf_ref[rows, gsl] + y_g) * sz_ref[rows, gsl].astype(jnp.float32)
                ms_ = jnp.mean(y * y, axis=-1, keepdims=True)
                o_ref[rows, gsl] = (y * lax.rsqrt(ms_ + NORM_EPS) * nw_ref[:, gsl]).astype(o_ref.dtype)
            else:
                o_ref[rows, gsl] = y_g + x_g * dskip_ref[:, gsl]


def _ssd_sweep(xs, bc, dt, zg, yf, alog, e_mat, row_w, batch, seq_len, rev):
    t_total = batch * seq_len
    q = SSD_STEP
    nc = seq_len // q
    rowmap = (lambda b, c: b * nc + nc - 1 - c) if rev else (lambda b, c: b * nc + c)
    row = lambda width: pl.BlockSpec((q, width), lambda b, c: (rowmap(b, c), 0))
    specs = [row(D_INNER), row(BC_W), row(LANES)]
    args = [xs, bc, dt]
    if rev:
        specs += [row(D_INNER), row(D_INNER)]
        args += [zg, yf]
    specs += [_const_spec(alog, 2), _const_spec(e_mat, 2), _const_spec(row_w, 2)]
    args += [alog, e_mat, row_w]
    return pl.pallas_call(
        functools.partial(_ssd_kernel, rev=rev),
        grid=(batch, nc),
        in_specs=specs,
        out_specs=row(D_INNER),
        out_shape=jax.ShapeDtypeStruct((t_total, D_INNER), MXU_DTYPE if rev else jnp.float32),
        scratch_shapes=[pltpu.VMEM((D_STATE, D_INNER), jnp.float32)],
        compiler_params=pltpu.CompilerParams(
            dimension_semantics=("parallel", "arbitrary"), vmem_limit_bytes=VMEM_LIMIT),
        name="ssd_bwd" if rev else "ssd_fwd",
    )(*args)


def _mix_kernel(x_ref, a0_ref, a1_ref, a2_ref, l0_ref, l1_ref, l2_ref, ssm_ref, sg_ref,
                wao_ref, wso_ref, wout_ref, o_ref, as_ref, ls_ref):
    tm = x_ref.shape[0]
    for gi, (a_ref, l_ref) in enumerate(((a1_ref, l1_ref), (a2_ref, l2_ref))):
        dil = ATT_PATTERNS[gi + 1][1]
        for r in range(dil):
            rows = pl.ds(r, tm // dil, stride=dil)
            a = a_ref[0, r].astype(jnp.float32)
            for h in range(HEADS_PER_GROUP):
                as_ref[gi, h, rows, :] = a[:, h * HEAD_DIM:(h + 1) * HEAD_DIM]
            ls_ref[gi, rows, :] = l_ref[0, r]
    l0, l1, l2 = l0_ref[0, 0], ls_ref[0], ls_ref[1]
    m = jnp.maximum(jnp.maximum(l0, l1), l2)
    e0, e1, e2 = jnp.exp(l0 - m), jnp.exp(l1 - m), jnp.exp(l2 - m)
    den = e0 + e1 + e2
    al0, al1, al2 = e0 / den, e1 / den, e2 / den
    heads = []
    for h in range(HEADS_PER_GROUP):
        hs = slice(h * HEAD_DIM, (h + 1) * HEAD_DIM)
        heads.append(al0[:, h:h + 1] * a0_ref[0, 0, :, hs].astype(jnp.float32)
                     + al1[:, h:h + 1] * as_ref[0, h] + al2[:, h:h + 1] * as_ref[1, h])
    att = jnp.concatenate(heads, axis=1)
    pa = jnp.dot(_mxu(att), wao_ref[...], preferred_element_type=jnp.float32)
    ps = jnp.dot(ssm_ref[...], wso_ref[...], preferred_element_type=jnp.float32)
    mixed = (sg_ref[:, :D_MODEL].astype(jnp.float32) * pa
             + sg_ref[:, D_MODEL:].astype(jnp.float32) * ps)
    o_ref[...] = x_ref[...] + jnp.dot(_mxu(mixed), wout_ref[...], preferred_element_type=jnp.float32)


def _mix(xf, att, lse, ssm, zg, wao, wso, wout, batch, seq_len):
    t_total = xf.shape[0]
    tm = 512
    tps = seq_len // tm
    row = lambda w, blk=0: pl.BlockSpec((tm, w), lambda i: (i, blk))
    cls = lambda dil, w: pl.BlockSpec((1, dil, tm // dil, w), lambda i: (i // tps, 0, i % tps, 0))
    dils = [d for _, d in ATT_PATTERNS]
    att4 = [a.reshape(batch, d, seq_len // d, ATT_OUT) for a, d in zip(att, dils)]
    lse4 = [l.reshape(batch, d, seq_len // d, LANES) for l, d in zip(lse, dils)]
    return pl.pallas_call(
        _mix_kernel,
        grid=(t_total // tm,),
        in_specs=[row(D_MODEL)] + [cls(d, ATT_OUT) for d in dils] + [cls(d, LANES) for d in dils]
        + [row(D_INNER), row(D_INNER, 1), _const_spec(wao, 1), _const_spec(wso, 1),
           _const_spec(wout, 1)],
        out_specs=row(D_MODEL),
        out_shape=jax.ShapeDtypeStruct((t_total, D_MODEL), jnp.float32),
        scratch_shapes=[pltpu.VMEM((ATT_GROUPS - 1, HEADS_PER_GROUP, tm, HEAD_DIM), jnp.float32),
                        pltpu.VMEM((ATT_GROUPS - 1, tm, LANES), jnp.float32)],
        compiler_params=pltpu.CompilerParams(
            dimension_semantics=("parallel",), vmem_limit_bytes=VMEM_LIMIT),
        name="mix_out",
    )(xf, *att4, *lse4, ssm, zg, wao, wso, wout)


FFN_CHUNK = 1408


def _ffn_kernel(x_ref, nw_ref, wgu_ref, wd_ref, fw_ref, o_ref, *, final_norm):
    x = x_ref[...]
    h = _rms_normed(x, nw_ref[...])
    acc = x
    for c0 in range(0, D_FF, FFN_CHUNK):
        gate = jnp.dot(h, wgu_ref[:, c0:c0 + FFN_CHUNK], preferred_element_type=jnp.float32)
        up = jnp.dot(h, wgu_ref[:, D_FF + c0:D_FF + c0 + FFN_CHUNK],
                     preferred_element_type=jnp.float32)
        a = _mxu(gate * _sigmoid(gate) * up)
        acc = acc + jnp.dot(a, wd_ref[c0:c0 + FFN_CHUNK, :], preferred_element_type=jnp.float32)
    if final_norm:
        ms2 = jnp.mean(acc * acc, axis=-1, keepdims=True)
        acc = acc * lax.rsqrt(ms2 + NORM_EPS) * fw_ref[...]
    o_ref[...] = acc


def _ffn(x1, norm_w, wgu, wd, final_w, final_norm):
    t_total = x1.shape[0]
    tm = 512
    row = pl.BlockSpec((tm, D_MODEL), lambda i: (i, 0))
    return pl.pallas_call(
        functools.partial(_ffn_kernel, final_norm=final_norm),
        grid=(t_total // tm,),
        in_specs=[row, _const_spec(norm_w, 1), _const_spec(wgu, 1), _const_spec(wd, 1),
                  _const_spec(final_w, 1)],
        out_specs=row,
        out_shape=jax.ShapeDtypeStruct((t_total, D_MODEL), jnp.float32),
        compiler_params=pltpu.CompilerParams(
            dimension_semantics=("parallel",), vmem_limit_bytes=VMEM_LIMIT),
        name="ffn",
    )(x1, norm_w, wgu, wd, final_w)


def _pad_lanes(a, width):
    return jnp.pad(a, [(0, 0)] * (a.ndim - 1) + [(0, width - a.shape[-1])])


def _rope_tables(seq_len):
    half = HEAD_DIM // 2
    inv = ROPE_THETA ** (-jnp.arange(half, dtype=jnp.float32) / half)
    ang = jnp.arange(seq_len).astype(jnp.float32)[:, None] * inv[None, :]
    cos, sin = jnp.cos(ang), jnp.sin(ang)
    return jnp.concatenate([cos, cos], axis=1), jnp.concatenate([-sin, sin], axis=1)


def _expand_matrix(row0):
    r = np.zeros((LANES, D_INNER), np.float32)
    for h in range(SSM_HEADS):
        r[row0 + h, h * SSM_HEAD_DIM:(h + 1) * SSM_HEAD_DIM] = 1.0
    return jnp.asarray(r, MXU_DTYPE)


def _prep_params(norm_mix, w_in, conv_w, conv_b, a_log, dt_bias, d_skip, ssm_norm,
                 w_attn_out, w_ssm_out, w_out, norm_ffn, w_gate_up, w_down, norm_final):
    b = [int(v) for v in np.cumsum([ATT_W, ATT_W, ATT_W, D_INNER, CONV_CH, 2 * SSM_HEADS, D_MODEL])]
    q, k, v = w_in[..., :b[0]], w_in[..., b[0]:b[1]], w_in[..., b[1]:b[2]]
    z, xbc, dt = w_in[..., b[2]:b[3]], w_in[..., b[3]:b[4]], w_in[..., b[4]:b[5]]
    ga, gm = w_in[..., b[5]:b[6]], w_in[..., b[6]:]
    grp = lambda t, g: t[..., g * ATT_OUT:(g + 1) * ATT_OUT]
    w_qkv = jnp.concatenate([grp(t, g) for g in range(ATT_GROUPS) for t in (q, k, v)], axis=-1)
    return {
        "norm_mix": norm_mix[:, None, :],
        "w_gates": _mxu(jnp.concatenate([z, ga, gm], axis=-1)),
        "w_xbc": _mxu(xbc),
        "w_dt": _mxu(_pad_lanes(dt, LANES)),
        "w_qkv": _mxu(w_qkv),
        "conv_w": jnp.pad(conv_w, ((0, 0), (0, SUBLANES - CONV_K), (0, 0))),
        "conv_b": conv_b[:, None, :],
        "a_log": _pad_lanes(a_log.reshape(DEPTH, 1, 2 * SSM_HEADS), LANES),
        "dt_bias": _pad_lanes(dt_bias.reshape(DEPTH, 1, 2 * SSM_HEADS), LANES),
        "d_skip": jnp.repeat(d_skip, SSM_HEAD_DIM, axis=-1)[:, None, :],
        "ssm_norm": ssm_norm[:, None, :],
        "w_attn_out": _mxu(w_attn_out),
        "w_ssm_out": _mxu(w_ssm_out),
        "w_out": _mxu(w_out),
        "norm_ffn": norm_ffn[:, None, :],
        "w_gate_up": _mxu(w_gate_up),
        "w_down": _mxu(w_down),
        "norm_final": norm_final[None, :],
        "e_fwd": _expand_matrix(0),
        "e_bwd": _expand_matrix(SSM_HEADS),
    }


def _trunk(x, p):
    batch, seq_len, _ = x.shape
    xf = x.reshape(batch * seq_len, D_MODEL)
    cos2, sin2 = _rope_tables(seq_len)
    for i in range(DEPTH):
        nw = p["norm_mix"][i]
        zg = _proj_gates(xf, nw, p["w_gates"][i])
        xs, bc, dt = _proj_xbc(xf, nw, p["w_xbc"][i], p["w_dt"][i], p["conv_w"][i], p["conv_b"][i],
                               p["dt_bias"][i], seq_len)
        qkv = _proj_qkv(xf, nw, p["w_qkv"][i], cos2, sin2, batch, seq_len)
        att, lse = [], []
        for g, (window, dil) in enumerate(ATT_PATTERNS):
            n = seq_len // dil
            o, l = _attn_group(qkv[g].reshape(batch * dil, 3, n, ATT_OUT), window // (2 * dil))
            att.append(o)
            lse.append(l)
        yf = _ssd_sweep(xs, bc, dt, None, None, p["a_log"][i], p["e_fwd"], p["d_skip"][i],
                        batch, seq_len, False)
        ssm = _ssd_sweep(xs, bc, dt, zg, yf, p["a_log"][i], p["e_bwd"], p["ssm_norm"][i],
                         batch, seq_len, True)
        x1 = _mix(xf, att, lse, ssm, zg, p["w_attn_out"][i], p["w_ssm_out"][i], p["w_out"][i],
                  batch, seq_len)
        xf = _ffn(x1, p["norm_ffn"][i], p["w_gate_up"][i], p["w_down"][i], p["norm_final"],
                  i == DEPTH - 1)
    return xf.reshape(batch, seq_len, D_MODEL)


def kernel(x_prompt, x_sample, norm_mix, w_in, conv_w, conv_b, a_log, dt_bias, d_skip, ssm_norm,
           w_attn_out, w_ssm_out, w_out, norm_ffn, w_gate_up, w_down, norm_final):
    p = _prep_params(norm_mix, w_in, conv_w, conv_b, a_log, dt_bias, d_skip, ssm_norm,
                     w_attn_out, w_ssm_out, w_out, norm_ffn, w_gate_up, w_down, norm_final)
    return (_trunk(x_prompt, p), _trunk(x_sample, p))
```

```python
import functools
import math

import jax
import jax.numpy as jnp
import numpy as np
from jax import lax
from jax.experimental import pallas as pl
from jax.experimental.pallas import tpu as pltpu

D_MODEL = 1024
DEPTH = 4
HEAD_DIM = 128
ATT_PATTERNS = ((128, 1), (512, 4), (2048, 16))
HEADS_PER_GROUP = 4
ATT_GROUPS = len(ATT_PATTERNS)
ATT_W = HEADS_PER_GROUP * ATT_GROUPS * HEAD_DIM
ATT_OUT = HEADS_PER_GROUP * HEAD_DIM
ROPE_THETA = 10000.0
D_INNER = 2048
SSM_HEAD_DIM = 64
SSM_HEADS = D_INNER // SSM_HEAD_DIM
SSM_GROUPS = 4
D_STATE = 128
CONV_K = 5
CONV_PAD = CONV_K // 2
BC_W = 2 * SSM_GROUPS * D_STATE
CONV_CH = D_INNER + BC_W
D_FF = 2816
NORM_EPS = 1e-6

MXU_DTYPE = jnp.bfloat16
LANES = 128
SUBLANES = 8
PACKED_ROWS = 16
SSD_CHUNK = 128
SSD_STEP = 256
LOG2_E = 1.4426950408889634
GROUP_W = D_INNER // SSM_GROUPS
HEADS_PER_SSM_GROUP = SSM_HEADS // SSM_GROUPS
ATT_SUB = 128
QKV_W = 3 * ATT_OUT

VMEM_LIMIT = 56 * 1024 * 1024


def _mxu(x):
    return x.astype(MXU_DTYPE)


def _sigmoid(x):
    return 1.0 / (1.0 + jnp.exp(-x))


def _split3(x):
    hi = _mxu(x)
    r1 = x - hi.astype(jnp.float32)
    mid = _mxu(r1)
    return hi, mid, _mxu(r1 - mid.astype(jnp.float32))


def _split_dot(lhs01, x):
    return sum(jnp.dot(lhs01, p, preferred_element_type=jnp.float32) for p in _split3(x))


def _split_dot_r(x, rhs01):
    return sum(jnp.dot(p, rhs01, preferred_element_type=jnp.float32) for p in _split3(x))


def _rms_normed(x, w):
    ms = jnp.mean(x * x, axis=-1, keepdims=True)
    return _mxu(x * lax.rsqrt(ms + NORM_EPS) * w)


def _const_spec(a, grid_rank):
    zeros = (0,) * a.ndim
    if grid_rank == 1:
        return pl.BlockSpec(a.shape, lambda i: zeros, pipeline_mode=pl.Buffered(1))
    return pl.BlockSpec(a.shape, lambda i, j: zeros, pipeline_mode=pl.Buffered(1))


GATES_CHUNK = 1024


def _proj_gates_kernel(x_ref, nw_ref, w_ref, o_ref, h_ref):
    h = _rms_normed(x_ref[...], nw_ref[...])
    h_ref[...] = h
    for c0 in range(0, w_ref.shape[1], GATES_CHUNK):
        cols = slice(c0, c0 + GATES_CHUNK)
        acc = jnp.dot(h, w_ref[:, cols], preferred_element_type=jnp.float32)
        s = _sigmoid(acc)
        o_ref[:, cols] = (acc * s if c0 < D_INNER else s).astype(o_ref.dtype)


def _proj_gates(xf, norm_w, w):
    t_total = xf.shape[0]
    tm = 1024
    row = lambda width: pl.BlockSpec((tm, width), lambda i: (i, 0))
    return pl.pallas_call(
        _proj_gates_kernel,
        grid=(t_total // tm,),
        in_specs=[row(D_MODEL), _const_spec(norm_w, 1), _const_spec(w, 1)],
        out_specs=[row(w.shape[1]), row(D_MODEL)],
        out_shape=[jax.ShapeDtypeStruct((t_total, w.shape[1]), MXU_DTYPE),
                   jax.ShapeDtypeStruct((t_total, D_MODEL), MXU_DTYPE)],
        compiler_params=pltpu.CompilerParams(
            dimension_semantics=("parallel",), vmem_limit_bytes=VMEM_LIMIT),
        name="proj_gates",
    )(xf, norm_w, w)


def _proj_xbc_kernel(h_ref, hp_ref, hn_ref, w_ref, wdt_ref, cw_ref, cbias_ref, dtb_ref,
                     xs_ref, bc_ref, dt_ref, hh_ref, acc_ref, *, tiles_per_seq):
    tm = h_ref.shape[0]
    hr = PACKED_ROWS
    iseq = pl.program_id(0) % tiles_per_seq
    hh_ref[0:hr, :] = hp_ref[...]
    hh_ref[hr:hr + tm, :] = h_ref[...]
    hh_ref[hr + tm:, :] = hn_ref[...]
    acc_ref[...] = jnp.dot(hh_ref[...], w_ref[...], preferred_element_type=jnp.float32)
    acc_ref[0:hr, :] = jnp.where(iseq > 0, acc_ref[0:hr, :], 0.0)
    acc_ref[hr + tm:, :] = jnp.where(iseq < tiles_per_seq - 1, acc_ref[hr + tm:, :], 0.0)
    for cc in range(CONV_CH // GROUP_W):
        sl = slice(cc * GROUP_W, (cc + 1) * GROUP_W)
        full = acc_ref[:, sl]
        a = cbias_ref[:, sl] + cw_ref[CONV_PAD:CONV_PAD + 1, sl] * full[hr:hr + tm]
        for k in range(CONV_K):
            if k != CONV_PAD:
                shifted = pltpu.roll(full, (CONV_PAD - k) % (tm + 2 * hr), axis=0)
                a = a + cw_ref[k:k + 1, sl] * shifted[hr:hr + tm]
        y = a * _sigmoid(a)
        if cc < D_INNER // GROUP_W:
            xs_ref[:, sl] = y
        else:
            bc_ref[:, cc * GROUP_W - D_INNER:(cc + 1) * GROUP_W - D_INNER] = _mxu(y)
    v = jnp.dot(h_ref[...], wdt_ref[...], preferred_element_type=jnp.float32) + dtb_ref[...]
    dt_ref[...] = jnp.maximum(v, 0.0) + jnp.log1p(jnp.exp(-jnp.abs(v)))


def _proj_xbc(h, w, wdt, conv_w8, conv_b, dtb, seq_len):
    t_total = h.shape[0]
    tm = 512
    hr = PACKED_ROWS
    per = tm // hr
    last = t_total // hr - 1
    row = lambda width: pl.BlockSpec((tm, width), lambda i: (i, 0))
    kern = functools.partial(_proj_xbc_kernel, tiles_per_seq=seq_len // tm)
    return pl.pallas_call(
        kern,
        grid=(t_total // tm,),
        in_specs=[
            row(D_MODEL),
            pl.BlockSpec((hr, D_MODEL), lambda i: (jnp.maximum(i * per - 1, 0), 0)),
            pl.BlockSpec((hr, D_MODEL), lambda i: (jnp.minimum((i + 1) * per, last), 0)),
            _const_spec(w, 1), _const_spec(wdt, 1),
            _const_spec(conv_w8, 1), _const_spec(conv_b, 1), _const_spec(dtb, 1),
        ],
        out_specs=[row(D_INNER), row(BC_W), row(LANES)],
        out_shape=[
            jax.ShapeDtypeStruct((t_total, D_INNER), jnp.float32),
            jax.ShapeDtypeStruct((t_total, BC_W), MXU_DTYPE),
            jax.ShapeDtypeStruct((t_total, LANES), jnp.float32),
        ],
        scratch_shapes=[
            pltpu.VMEM((tm + 2 * hr, D_MODEL), MXU_DTYPE),
            pltpu.VMEM((tm + 2 * hr, CONV_CH), jnp.float32),
        ],
        compiler_params=pltpu.CompilerParams(
            dimension_semantics=("parallel",), vmem_limit_bytes=VMEM_LIMIT),
        name="proj_xbc",
    )(h, h, h, w, wdt, conv_w8, conv_b, dtb)


PERM_BLOCK = 256


def _proj_qkv_kernel(h_ref, w_ref, cos_ref, sin_ref, perm_ref, o0_ref, o1_ref, o2_ref, r_ref):
    tm = h_ref.shape[0]
    j = pl.program_id(1)
    acc = jnp.dot(h_ref[...], w_ref[...], preferred_element_type=jnp.float32)
    cos = cos_ref[...]
    sin = sin_ref[...]
    n_cols = QKV_W // LANES
    for c in range(n_cols):
        t = acc[:, c * LANES:(c + 1) * LANES]
        if c < 2 * HEADS_PER_GROUP:
            t = t * cos + pltpu.roll(t, HEAD_DIM // 2, axis=1) * sin
        r_ref[c] = t

    @pl.when(j == 0)
    def _():
        for c in range(n_cols):
            which, hh = divmod(c, HEADS_PER_GROUP)
            o0_ref[0, 0, which, :, hh * LANES:(hh + 1) * LANES] = _mxu(r_ref[c])

    dil1 = ATT_PATTERNS[1][1]

    @pl.when(j == 1)
    def _():
        for r in range(dil1):
            for c in range(n_cols):
                which, hh = divmod(c, HEADS_PER_GROUP)
                o1_ref[0, r, which, :, hh * LANES:(hh + 1) * LANES] = _mxu(
                    r_ref[c, pl.ds(r, tm // dil1, stride=dil1), :])

    dil2 = ATT_PATTERNS[2][1]
    per_class = PERM_BLOCK // dil2

    @pl.when(j == 2)
    def _():
        perm = perm_ref[...]
        for blk in range(tm // PERM_BLOCK):
            rows = slice(blk * PERM_BLOCK, (blk + 1) * PERM_BLOCK)
            tb = jnp.concatenate([_mxu(r_ref[c, rows, :]) for c in range(n_cols)], axis=1)
            pb = _mxu(jnp.dot(perm, tb, preferred_element_type=jnp.float32))
            for r in range(dil2):
                for which in range(3):
                    o2_ref[0, r, which, blk * per_class:(blk + 1) * per_class, :] = (
                        pb[r * per_class:(r + 1) * per_class, which * ATT_OUT:(which + 1) * ATT_OUT])


def _class_permutation(dil):
    p = np.zeros((PERM_BLOCK, PERM_BLOCK), np.float32)
    per_class = PERM_BLOCK // dil
    for r in range(dil):
        for jj in range(per_class):
            p[r * per_class + jj, jj * dil + r] = 1.0
    return jnp.asarray(p, MXU_DTYPE)


def _proj_qkv(h, w, cos2, sin2, batch, seq_len):
    t_total = h.shape[0]
    tm = 1024
    tps = seq_len // tm
    perm = _class_permutation(ATT_PATTERNS[2][1])
    out_specs, out_shape = [], []
    for _, dil in ATT_PATTERNS:
        out_specs.append(pl.BlockSpec((1, dil, 3, tm // dil, ATT_OUT),
                                      lambda i, j: (i // tps, 0, 0, i % tps, 0)))
        out_shape.append(jax.ShapeDtypeStruct((batch, dil, 3, seq_len // dil, ATT_OUT), MXU_DTYPE))
    return pl.pallas_call(
        _proj_qkv_kernel,
        grid=(t_total // tm, ATT_GROUPS),
        in_specs=[
            pl.BlockSpec((tm, D_MODEL), lambda i, j: (i, 0)),
            pl.BlockSpec((D_MODEL, QKV_W), lambda i, j: (0, j)),
            pl.BlockSpec((tm, HEAD_DIM), lambda i, j: (i % tps, 0)),
            pl.BlockSpec((tm, HEAD_DIM), lambda i, j: (i % tps, 0)),
            _const_spec(perm, 2),
        ],
        out_specs=out_specs,
        out_shape=out_shape,
        scratch_shapes=[pltpu.VMEM((QKV_W // LANES, tm, LANES), jnp.float32)],
        compiler_params=pltpu.CompilerParams(
            dimension_semantics=("parallel", "arbitrary"), vmem_limit_bytes=VMEM_LIMIT),
        name="proj_qkv",
    )(h, w, cos2, sin2, perm)


def _attn_kernel(q_ref, kc_ref, kp_ref, kn_ref, vc_ref, vp_ref, vn_ref, o_ref, lse_ref,
                 kbuf, vbuf, *, n, tq, radius):
    i = pl.program_id(1)
    for buf, p_ref, c_ref, n_ref in ((kbuf, kp_ref, kc_ref, kn_ref), (vbuf, vp_ref, vc_ref, vn_ref)):
        buf[0:radius, :] = p_ref[0, 0]
        buf[radius:radius + tq, :] = c_ref[0, 0]
        buf[radius + tq:, :] = n_ref[0, 0]
    sub = ATT_SUB
    win = sub + 2 * radius
    row = lax.broadcasted_iota(jnp.int32, (sub, win), 0)
    col = lax.broadcasted_iota(jnp.int32, (sub, win), 1)
    in_band = jnp.abs(col - row - radius) <= radius
    lane = lax.broadcasted_iota(jnp.int32, (sub, LANES), 1)

    def body(t, carry):
        r0 = pl.multiple_of(t * sub, sub)
        kpos = i * tq + r0 - radius + col
        in_seq = kpos.astype(jnp.uint32) < jnp.uint32(n)
        bias = jnp.where(in_band, jnp.where(in_seq, 0.0, -jnp.inf), -jnp.inf)
        lse_tile = jnp.zeros((sub, LANES), jnp.float32)
        for h in range(HEADS_PER_GROUP):
            hs = slice(h * HEAD_DIM, (h + 1) * HEAD_DIM)
            q = q_ref[0, 0, pl.ds(r0, sub), hs]
            kw = kbuf[pl.ds(r0, win), hs]
            vw = vbuf[pl.ds(r0, win), hs]
            s = lax.dot_general(q, kw, (((1,), (1,)), ((), ())), preferred_element_type=jnp.float32)
            s = s * (1.0 / math.sqrt(HEAD_DIM)) + bias
            m = jnp.max(s, axis=-1, keepdims=True)
            e = jnp.exp(s - m)
            den = jnp.sum(e, axis=-1, keepdims=True)
            o = jnp.dot(_mxu(e), vw, preferred_element_type=jnp.float32) / den
            o_ref[0, pl.ds(r0, sub), hs] = o.astype(o_ref.dtype)
            lse_tile = jnp.where(lane == h, m + jnp.log(den), lse_tile)
        lse_ref[0, pl.ds(r0, sub), :] = lse_tile
        return carry

    lax.fori_loop(0, tq // sub, body, 0, unroll=min(4, tq // sub))


def _attn_group(qkv, radius):
    s_total, _, n, _ = qkv.shape
    tq = min(1024, n)
    per = tq // radius
    last = n // radius - 1
    kern = functools.partial(_attn_kernel, n=n, tq=tq, radius=radius)
    cur = lambda which: pl.BlockSpec((1, 1, tq, ATT_OUT), lambda s, i: (s, which, i, 0))
    prev = lambda which: pl.BlockSpec(
        (1, 1, radius, ATT_OUT), lambda s, i: (s, which, jnp.maximum(i * per - 1, 0), 0))
    nxt = lambda which: pl.BlockSpec(
        (1, 1, radius, ATT_OUT), lambda s, i: (s, which, jnp.minimum((i + 1) * per, last), 0))
    return pl.pallas_call(
        kern,
        grid=(s_total, n // tq),
        in_specs=[cur(0), cur(1), prev(1), nxt(1), cur(2), prev(2), nxt(2)],
        out_specs=[
            pl.BlockSpec((1, tq, ATT_OUT), lambda s, i: (s, i, 0)),
            pl.BlockSpec((1, tq, LANES), lambda s, i: (s, i, 0)),
        ],
        out_shape=[
            jax.ShapeDtypeStruct((s_total, n, ATT_OUT), MXU_DTYPE),
            jax.ShapeDtypeStruct((s_total, n, LANES), jnp.float32),
        ],
        scratch_shapes=[pltpu.VMEM((tq + 2 * radius, ATT_OUT), MXU_DTYPE),
                        pltpu.VMEM((tq + 2 * radius, ATT_OUT), MXU_DTYPE)],
        compiler_params=pltpu.CompilerParams(
            dimension_semantics=("parallel", "arbitrary"), vmem_limit_bytes=VMEM_LIMIT),
        name="band_attn",
    )(qkv, qkv, qkv, qkv, qkv, qkv, qkv)


def _ssd_kernel(*refs, rev):
    if rev:
        xs_ref, bc_ref, dt_ref, sz_ref, yf_ref, alog_ref, e_ref, nw_ref, o_ref, st_ref = refs
    else:
        xs_ref, bc_ref, dt_ref, alog_ref, e_ref, dskip_ref, o_ref, st_ref = refs
    q = SSD_CHUNK

    @pl.when(pl.program_id(1) == 0)
    def _():
        st_ref[...] = jnp.zeros_like(st_ref)

    row = lax.broadcasted_iota(jnp.int32, (q, q), 0)
    col = lax.broadcasted_iota(jnp.int32, (q, q), 1)
    inside = (col >= row) if rev else (col <= row)
    tri = jnp.where(inside, 1.0, 0.0).astype(MXU_DTYPE)
    lane = lax.broadcasted_iota(jnp.int32, (q, LANES), 1)
    lane_off = SSM_HEADS if rev else 0
    e_mat = e_ref[...]
    neg_a = jnp.exp(alog_ref[...])
    expand = lambda f: jnp.dot(_mxu(f), e_mat, preferred_element_type=jnp.float32)
    chunks = range(SSD_STEP // q)
    gsls = [slice(g * GROUP_W, (g + 1) * GROUP_W) for g in range(SSM_GROUPS)]

    for ci in (reversed(chunks) if rev else chunks):
        rows = slice(ci * q, (ci + 1) * q)
        b_gs = [bc_ref[rows, g * D_STATE:(g + 1) * D_STATE] for g in range(SSM_GROUPS)]
        c_gs = [bc_ref[rows, GROUP_W + g * D_STATE:GROUP_W + (g + 1) * D_STATE]
                for g in range(SSM_GROUPS)]
        cbs = [jnp.where(inside, lax.dot_general(c_gs[g], b_gs[g], (((1,), (1,)), ((), ())),
                                                 preferred_element_type=jnp.float32), 0.0)
               for g in range(SSM_GROUPS)]
        y_offs = [jnp.dot(c_gs[g], _mxu(st_ref[:, gsls[g]]), preferred_element_type=jnp.float32)
                  for g in range(SSM_GROUPS)]
        dt = dt_ref[rows, :]
        acum = _split_dot(tri, dt * (-neg_a))
        tot = acum[0:1, :] if rev else acum[q - 1:q, :]
        acum2 = acum * LOG2_E
        acum2_t = acum2.T
        dt_t = dt.T
        e_acc = jnp.exp(acum)
        ex_acc = expand(e_acc)
        ex_w = expand(jnp.exp(tot - acum) * dt)
        edge = _split_dot_r(e_acc[0:SUBLANES, :] if rev else e_acc[q - SUBLANES:, :], e_mat)
        dec_row = edge[0:1, :] if rev else edge[SUBLANES - 1:, :]

        for g in range(SSM_GROUPS):
            gsl = gsls[g]
            x_g = xs_ref[rows, gsl]
            parts = []
            for jj in range(HEADS_PER_SSM_GROUP // 2):
                ms = []
                for h in (g * HEADS_PER_SSM_GROUP + 2 * jj, g * HEADS_PER_SSM_GROUP + 2 * jj + 1):
                    hl = lane_off + h
                    seg2 = jnp.minimum(acum2[:, hl:hl + 1] - acum2_t[hl:hl + 1, :], 0.0)
                    ms.append(cbs[g] * jnp.exp2(seg2) * dt_t[hl:hl + 1, :])
                m_pair = _mxu(jnp.concatenate(ms, axis=1))
                xp = x_g[:, jj * LANES:(jj + 1) * LANES]
                r_pair = _mxu(jnp.concatenate(
                    [jnp.where(lane < SSM_HEAD_DIM, xp, 0.0), jnp.where(lane >= SSM_HEAD_DIM, xp, 0.0)],
                    axis=0))
                parts.append(jnp.dot(m_pair, r_pair, preferred_element_type=jnp.float32))
            y_g = y_offs[g] * ex_acc[:, gsl] + jnp.concatenate(parts, axis=1)
            b_t = _mxu(b_gs[g].astype(jnp.float32).T)
            s_new = jnp.dot(b_t, _mxu(x_g * ex_w[:, gsl]), preferred_element_type=jnp.float32)
            st_ref[:, gsl] = st_ref[:, gsl] * dec_row[:, gsl] + s_new
            if rev:
                y = (yf_ref[rows, gsl] + y_g) * sz_ref[rows, gsl].astype(jnp.float32)
                ms_ = jnp.mean(y * y, axis=-1, keepdims=True)
                o_ref[rows, gsl] = (y * lax.rsqrt(ms_ + NORM_EPS) * nw_ref[:, gsl]).astype(o_ref.dtype)
            else:
                o_ref[rows, gsl] = y_g + x_g * dskip_ref[:, gsl]


def _ssd_sweep(xs, bc, dt, zg, yf, alog, e_mat, row_w, batch, seq_len, rev):
    t_total = batch * seq_len
    q = SSD_STEP
    nc = seq_len // q
    rowmap = (lambda b, c: b * nc + nc - 1 - c) if rev else (lambda b, c: b * nc + c)
    row = lambda width: pl.BlockSpec((q, width), lambda b, c: (rowmap(b, c), 0))
    specs = [row(D_INNER), row(BC_W), row(LANES)]
    args = [xs, bc, dt]
    if rev:
        specs += [row(D_INNER), row(D_INNER)]
        args += [zg, yf]
    specs += [_const_spec(alog, 2), _const_spec(e_mat, 2), _const_spec(row_w, 2)]
    args += [alog, e_mat, row_w]
    return pl.pallas_call(
        functools.partial(_ssd_kernel, rev=rev),
        grid=(batch, nc),
        in_specs=specs,
        out_specs=row(D_INNER),
        out_shape=jax.ShapeDtypeStruct((t_total, D_INNER), MXU_DTYPE if rev else jnp.float32),
        scratch_shapes=[pltpu.VMEM((D_STATE, D_INNER), jnp.float32)],
        compiler_params=pltpu.CompilerParams(
            dimension_semantics=("parallel", "arbitrary"), vmem_limit_bytes=VMEM_LIMIT),
        name="ssd_bwd" if rev else "ssd_fwd",
    )(*args)


def _mix_kernel(x_ref, a0_ref, a1_ref, a2_ref, l0_ref, l1_ref, l2_ref, ssm_ref, sg_ref,
                wao_ref, wso_ref, wout_ref, o_ref, as_ref, ls_ref):
    tm = x_ref.shape[0]
    ps = jnp.dot(ssm_ref[...], wso_ref[...], preferred_element_type=jnp.float32)
    for gi, (a_ref, l_ref) in enumerate(((a1_ref, l1_ref), (a2_ref, l2_ref))):
        dil = ATT_PATTERNS[gi + 1][1]
        for r in range(dil):
            rows = pl.ds(r, tm // dil, stride=dil)
            a = a_ref[0, r].astype(jnp.float32)
            for h in range(HEADS_PER_GROUP):
                as_ref[gi, h, rows, :] = a[:, h * HEAD_DIM:(h + 1) * HEAD_DIM]
            ls_ref[gi, rows, :] = l_ref[0, r]
    l0, l1, l2 = l0_ref[0, 0], ls_ref[0], ls_ref[1]
    m = jnp.maximum(jnp.maximum(l0, l1), l2)
    e0, e1, e2 = jnp.exp(l0 - m), jnp.exp(l1 - m), jnp.exp(l2 - m)
    den = e0 + e1 + e2
    al0, al1, al2 = e0 / den, e1 / den, e2 / den
    heads = []
    for h in range(HEADS_PER_GROUP):
        hs = slice(h * HEAD_DIM, (h + 1) * HEAD_DIM)
        heads.append(al0[:, h:h + 1] * a0_ref[0, 0, :, hs].astype(jnp.float32)
                     + al1[:, h:h + 1] * as_ref[0, h] + al2[:, h:h + 1] * as_ref[1, h])
    att = jnp.concatenate(heads, axis=1)
    pa = jnp.dot(_mxu(att), wao_ref[...], preferred_element_type=jnp.float32)
    mixed = (sg_ref[:, :D_MODEL].astype(jnp.float32) * pa
             + sg_ref[:, D_MODEL:].astype(jnp.float32) * ps)
    o_ref[...] = x_ref[...] + jnp.dot(_mxu(mixed), wout_ref[...], preferred_element_type=jnp.float32)


def _mix(xf, att, lse, ssm, zg, wao, wso, wout, batch, seq_len):
    t_total = xf.shape[0]
    tm = 512
    tps = seq_len // tm
    row = lambda w, blk=0: pl.BlockSpec((tm, w), lambda i: (i, blk))
    cls = lambda dil, w: pl.BlockSpec((1, dil, tm // dil, w), lambda i: (i // tps, 0, i % tps, 0))
    dils = [d for _, d in ATT_PATTERNS]
    att4 = [a.reshape(batch, d, seq_len // d, ATT_OUT) for a, d in zip(att, dils)]
    lse4 = [l.reshape(batch, d, seq_len // d, LANES) for l, d in zip(lse, dils)]
    return pl.pallas_call(
        _mix_kernel,
        grid=(t_total // tm,),
        in_specs=[row(D_MODEL)] + [cls(d, ATT_OUT) for d in dils] + [cls(d, LANES) for d in dils]
        + [row(D_INNER), row(D_INNER, 1), _const_spec(wao, 1), _const_spec(wso, 1),
           _const_spec(wout, 1)],
        out_specs=row(D_MODEL),
        out_shape=jax.ShapeDtypeStruct((t_total, D_MODEL), jnp.float32),
        scratch_shapes=[pltpu.VMEM((ATT_GROUPS - 1, HEADS_PER_GROUP, tm, HEAD_DIM), jnp.float32),
                        pltpu.VMEM((ATT_GROUPS - 1, tm, LANES), jnp.float32)],
        compiler_params=pltpu.CompilerParams(
            dimension_semantics=("parallel",), vmem_limit_bytes=VMEM_LIMIT),
        name="mix_out",
    )(xf, *att4, *lse4, ssm, zg, wao, wso, wout)


FFN_CHUNK = 1408


def _ffn_kernel(x_ref, nw_ref, wgu_ref, wd_ref, fw_ref, o_ref, *, final_norm):
    x = x_ref[...]
    h = _rms_normed(x, nw_ref[...])
    acc = x
    for c0 in range(0, D_FF, FFN_CHUNK):
        gate = jnp.dot(h, wgu_ref[:, c0:c0 + FFN_CHUNK], preferred_element_type=jnp.float32)
        up = jnp.dot(h, wgu_ref[:, D_FF + c0:D_FF + c0 + FFN_CHUNK],
                     preferred_element_type=jnp.float32)
        a = _mxu(gate * _sigmoid(gate) * up)
        acc = acc + jnp.dot(a, wd_ref[c0:c0 + FFN_CHUNK, :], preferred_element_type=jnp.float32)
    if final_norm:
        ms2 = jnp.mean(acc * acc, axis=-1, keepdims=True)
        acc = acc * lax.rsqrt(ms2 + NORM_EPS) * fw_ref[...]
    o_ref[...] = acc


def _ffn(x1, norm_w, wgu, wd, final_w, final_norm):
    t_total = x1.shape[0]
    tm = 512
    row = pl.BlockSpec((tm, D_MODEL), lambda i: (i, 0))
    return pl.pallas_call(
        functools.partial(_ffn_kernel, final_norm=final_norm),
        grid=(t_total // tm,),
        in_specs=[row, _const_spec(norm_w, 1), _const_spec(wgu, 1), _const_spec(wd, 1),
                  _const_spec(final_w, 1)],
        out_specs=row,
        out_shape=jax.ShapeDtypeStruct((t_total, D_MODEL), jnp.float32),
        compiler_params=pltpu.CompilerParams(
            dimension_semantics=("parallel",), vmem_limit_bytes=VMEM_LIMIT),
        name="ffn",
    )(x1, norm_w, wgu, wd, final_w)


def _pad_lanes(a, width):
    return jnp.pad(a, [(0, 0)] * (a.ndim - 1) + [(0, width - a.shape[-1])])


def _rope_tables(seq_len):
    half = HEAD_DIM // 2
    inv = ROPE_THETA ** (-jnp.arange(half, dtype=jnp.float32) / half)
    ang = jnp.arange(seq_len).astype(jnp.float32)[:, None] * inv[None, :]
    cos, sin = jnp.cos(ang), jnp.sin(ang)
    return jnp.concatenate([cos, cos], axis=1), jnp.concatenate([-sin, sin], axis=1)


def _expand_matrix(row0):
    r = np.zeros((LANES, D_INNER), np.float32)
    for h in range(SSM_HEADS):
        r[row0 + h, h * SSM_HEAD_DIM:(h + 1) * SSM_HEAD_DIM] = 1.0
    return jnp.asarray(r, MXU_DTYPE)


def _prep_params(norm_mix, w_in, conv_w, conv_b, a_log, dt_bias, d_skip, ssm_norm,
                 w_attn_out, w_ssm_out, w_out, norm_ffn, w_gate_up, w_down, norm_final):
    b = [int(v) for v in np.cumsum([ATT_W, ATT_W, ATT_W, D_INNER, CONV_CH, 2 * SSM_HEADS, D_MODEL])]
    q, k, v = w_in[..., :b[0]], w_in[..., b[0]:b[1]], w_in[..., b[1]:b[2]]
    z, xbc, dt = w_in[..., b[2]:b[3]], w_in[..., b[3]:b[4]], w_in[..., b[4]:b[5]]
    ga, gm = w_in[..., b[5]:b[6]], w_in[..., b[6]:]
    grp = lambda t, g: t[..., g * ATT_OUT:(g + 1) * ATT_OUT]
    w_qkv = jnp.concatenate([grp(t, g) for g in range(ATT_GROUPS) for t in (q, k, v)], axis=-1)
    return {
        "norm_mix": norm_mix[:, None, :],
        "w_gates": _mxu(jnp.concatenate([z, ga, gm], axis=-1)),
        "w_xbc": _mxu(xbc),
        "w_dt": _mxu(_pad_lanes(dt, LANES)),
        "w_qkv": _mxu(w_qkv),
        "conv_w": jnp.pad(conv_w, ((0, 0), (0, SUBLANES - CONV_K), (0, 0))),
        "conv_b": conv_b[:, None, :],
        "a_log": _pad_lanes(a_log.reshape(DEPTH, 1, 2 * SSM_HEADS), LANES),
        "dt_bias": _pad_lanes(dt_bias.reshape(DEPTH, 1, 2 * SSM_HEADS), LANES),
        "d_skip": jnp.repeat(d_skip, SSM_HEAD_DIM, axis=-1)[:, None, :],
        "ssm_norm": ssm_norm[:, None, :],
        "w_attn_out": _mxu(w_attn_out),
        "w_ssm_out": _mxu(w_ssm_out),
        "w_out": _mxu(w_out),
        "norm_ffn": norm_ffn[:, None, :],
        "w_gate_up": _mxu(w_gate_up),
        "w_down": _mxu(w_down),
        "norm_final": norm_final[None, :],
        "e_fwd": _expand_matrix(0),
        "e_bwd": _expand_matrix(SSM_HEADS),
    }


def _trunk(x, p):
    batch, seq_len, _ = x.shape
    xf = x.reshape(batch * seq_len, D_MODEL)
    cos2, sin2 = _rope_tables(seq_len)
    for i in range(DEPTH):
        nw = p["norm_mix"][i]
        zg, h = _proj_gates(xf, nw, p["w_gates"][i])
        xs, bc, dt = _proj_xbc(h, p["w_xbc"][i], p["w_dt"][i], p["conv_w"][i], p["conv_b"][i],
                               p["dt_bias"][i], seq_len)
        qkv = _proj_qkv(h, p["w_qkv"][i], cos2, sin2, batch, seq_len)
        att, lse = [], []
        for g, (window, dil) in enumerate(ATT_PATTERNS):
            n = seq_len // dil
            o, l = _attn_group(qkv[g].reshape(batch * dil, 3, n, ATT_OUT), window // (2 * dil))
            att.append(o)
            lse.append(l)
        yf = _ssd_sweep(xs, bc, dt, None, None, p["a_log"][i], p["e_fwd"], p["d_skip"][i],
                        batch, seq_len, False)
        ssm = _ssd_sweep(xs, bc, dt, zg, yf, p["a_log"][i], p["e_bwd"], p["ssm_norm"][i],
                         batch, seq_len, True)
        x1 = _mix(xf, att, lse, ssm, zg, p["w_attn_out"][i], p["w_ssm_out"][i], p["w_out"][i],
                  batch, seq_len)
        xf = _ffn(x1, p["norm_ffn"][i], p["w_gate_up"][i], p["w_down"][i], p["norm_final"],
                  i == DEPTH - 1)
    return xf.reshape(batch, seq_len, D_MODEL)


def kernel(x_prompt, x_sample, norm_mix, w_in, conv_w, conv_b, a_log, dt_bias, d_skip, ssm_norm,
           w_attn_out, w_ssm_out, w_out, norm_ffn, w_gate_up, w_down, norm_final):
    p = _prep_params(norm_mix, w_in, conv_w, conv_b, a_log, dt_bias, d_skip, ssm_norm,
                     w_attn_out, w_ssm_out, w_out, norm_ffn, w_gate_up, w_down, norm_final)
    return (_trunk(x_prompt, p), _trunk(x_sample, p))
```

```python
import functools
import math

import jax
import jax.numpy as jnp
import numpy as np
from jax import lax
from jax.experimental import pallas as pl
from jax.experimental.pallas import tpu as pltpu

D_MODEL = 1024
DEPTH = 4
HEAD_DIM = 128
ATT_PATTERNS = ((128, 1), (512, 4), (2048, 16))
HEADS_PER_GROUP = 4
ATT_GROUPS = len(ATT_PATTERNS)
ATT_W = HEADS_PER_GROUP * ATT_GROUPS * HEAD_DIM
ATT_OUT = HEADS_PER_GROUP * HEAD_DIM
ROPE_THETA = 10000.0
D_INNER = 2048
SSM_HEAD_DIM = 64
SSM_HEADS = D_INNER // SSM_HEAD_DIM
SSM_GROUPS = 4
D_STATE = 128
CONV_K = 5
CONV_PAD = CONV_K // 2
BC_W = 2 * SSM_GROUPS * D_STATE
CONV_CH = D_INNER + BC_W
D_FF = 2816
NORM_EPS = 1e-6

MXU_DTYPE = jnp.bfloat16
LANES = 128
SUBLANES = 8
PACKED_ROWS = 16
SSD_CHUNK = 128
SSD_STEP = 512
LOG2_E = 1.4426950408889634
GROUP_W = D_INNER // SSM_GROUPS
HEADS_PER_SSM_GROUP = SSM_HEADS // SSM_GROUPS
ATT_SUB = 128
QKV_W = 3 * ATT_OUT

VMEM_LIMIT = 56 * 1024 * 1024


def _mxu(x):
    return x.astype(MXU_DTYPE)


def _sigmoid(x):
    return 1.0 / (1.0 + jnp.exp(-x))


def _split3(x):
    hi = _mxu(x)
    r1 = x - hi.astype(jnp.float32)
    mid = _mxu(r1)
    return hi, mid, _mxu(r1 - mid.astype(jnp.float32))


def _split_dot(lhs01, x):
    return sum(jnp.dot(lhs01, p, preferred_element_type=jnp.float32) for p in _split3(x))


def _split_dot_r(x, rhs01):
    return sum(jnp.dot(p, rhs01, preferred_element_type=jnp.float32) for p in _split3(x))


def _rms_normed(x, w):
    ms = jnp.mean(x * x, axis=-1, keepdims=True)
    return _mxu(x * lax.rsqrt(ms + NORM_EPS) * w)


def _const_spec(a, grid_rank):
    zeros = (0,) * a.ndim
    if grid_rank == 1:
        return pl.BlockSpec(a.shape, lambda i: zeros, pipeline_mode=pl.Buffered(1))
    return pl.BlockSpec(a.shape, lambda i, j: zeros, pipeline_mode=pl.Buffered(1))


GATES_CHUNK = 1024


def _proj_gates_kernel(x_ref, nw_ref, w_ref, o_ref, h_ref):
    h = _rms_normed(x_ref[...], nw_ref[...])
    h_ref[...] = h
    for c0 in range(0, w_ref.shape[1], GATES_CHUNK):
        cols = slice(c0, c0 + GATES_CHUNK)
        acc = jnp.dot(h, w_ref[:, cols], preferred_element_type=jnp.float32)
        s = _sigmoid(acc)
        o_ref[:, cols] = (acc * s if c0 < D_INNER else s).astype(o_ref.dtype)


def _proj_gates(xf, norm_w, w):
    t_total = xf.shape[0]
    tm = 1024
    row = lambda width: pl.BlockSpec((tm, width), lambda i: (i, 0))
    return pl.pallas_call(
        _proj_gates_kernel,
        grid=(t_total // tm,),
        in_specs=[row(D_MODEL), _const_spec(norm_w, 1), _const_spec(w, 1)],
        out_specs=[row(w.shape[1]), row(D_MODEL)],
        out_shape=[jax.ShapeDtypeStruct((t_total, w.shape[1]), MXU_DTYPE),
                   jax.ShapeDtypeStruct((t_total, D_MODEL), MXU_DTYPE)],
        compiler_params=pltpu.CompilerParams(
            dimension_semantics=("parallel",), vmem_limit_bytes=VMEM_LIMIT),
        name="proj_gates",
    )(xf, norm_w, w)


CONV_BLOCK = 256
CONV_SEG = CONV_BLOCK // SUBLANES


def _proj_xbc_kernel(h_ref, hp_ref, hn_ref, w_ref, wdt_ref, cw_ref, cbias_ref, dtb_ref, pin_ref,
                     pout_ref, xs_ref, bc_ref, dt_ref, hh_ref, acc_ref, *, tiles_per_seq):
    tm = h_ref.shape[0]
    hr = PACKED_ROWS
    pb = CONV_BLOCK
    nb = tm // pb
    iseq = pl.program_id(0) % tiles_per_seq
    pin = pin_ref[...]
    for b in range(nb):
        blk = slice(b * pb, (b + 1) * pb)
        hh_ref[blk, :] = _mxu(jnp.dot(pin, h_ref[blk, :], preferred_element_type=jnp.float32))
    hh_ref[tm:tm + hr, :] = hp_ref[...]
    hh_ref[tm + hr:, :] = hn_ref[...]
    acc_ref[...] = jnp.dot(hh_ref[...], w_ref[...], preferred_element_type=jnp.float32)
    pout = pout_ref[...]
    s8 = SUBLANES
    for cc in range(CONV_CH // GROUP_W):
        sl = slice(cc * GROUP_W, (cc + 1) * GROUP_W)
        sub = lax.broadcasted_iota(jnp.int32, (s8, GROUP_W), 0)
        before = jnp.where(iseq > 0, acc_ref[tm + hr - s8:tm + hr, sl], 0.0)
        after = jnp.where(iseq < tiles_per_seq - 1, acc_ref[tm + hr:tm + hr + s8, sl], 0.0)
        blocks = [acc_ref[b * pb:(b + 1) * pb, sl] for b in range(nb)]
        down = lambda v, n=1: pltpu.roll(v, n, axis=0)
        up = lambda v, n=1: pltpu.roll(v, s8 - n, axis=0)
        for b in range(nb):
            cur = blocks[b]
            if b == 0:
                fill_m1, fill_m2 = down(before, 1), down(before, 2)
            else:
                fill_m1, fill_m2 = down(blocks[b - 1][pb - s8:]), down(blocks[b - 1][pb - 2 * s8:pb - s8])
            if b == nb - 1:
                fill_p0, fill_p1 = up(after, 1), up(after, 2)
            else:
                fill_p0, fill_p1 = up(blocks[b + 1][:s8]), up(blocks[b + 1][s8:2 * s8])
            e_m1 = jnp.where(sub == 0, fill_m1, down(cur[pb - s8:]))
            e_m2 = jnp.where(sub == 0, fill_m2, down(cur[pb - 2 * s8:pb - s8]))
            e_p0 = jnp.where(sub == s8 - 1, fill_p0, up(cur[:s8]))
            e_p1 = jnp.where(sub == s8 - 1, fill_p1, up(cur[s8:2 * s8]))
            ext = jnp.concatenate([e_m2, e_m1, cur, e_p0, e_p1], axis=0)
            a = cbias_ref[:, sl]
            for k in range(CONV_K):
                a = a + cw_ref[k:k + 1, sl] * ext[k * s8:k * s8 + pb]
            y = _mxu(a * _sigmoid(a))
            y = jnp.dot(pout, y, preferred_element_type=jnp.float32).astype(MXU_DTYPE)
            blk = slice(b * pb, (b + 1) * pb)
            if cc < D_INNER // GROUP_W:
                xs_ref[blk, sl] = y
            else:
                bc_ref[blk, cc * GROUP_W - D_INNER:(cc + 1) * GROUP_W - D_INNER] = y
    v = jnp.dot(h_ref[...], wdt_ref[...], preferred_element_type=jnp.float32) + dtb_ref[...]
    dt_ref[...] = jnp.maximum(v, 0.0) + jnp.log1p(jnp.exp(-jnp.abs(v)))


def _conv_permutation():
    p = np.zeros((CONV_BLOCK, CONV_BLOCK), np.float32)
    for t in range(CONV_BLOCK):
        p[(t % CONV_SEG) * SUBLANES + t // CONV_SEG, t] = 1.0
    return p


def _proj_xbc(h, w, wdt, conv_w8, conv_b, dtb, seq_len):
    t_total = h.shape[0]
    tm = 512
    hr = PACKED_ROWS
    per = tm // hr
    last = t_total // hr - 1
    p = _conv_permutation()
    pin, pout = jnp.asarray(p, MXU_DTYPE), jnp.asarray(p.T, MXU_DTYPE)
    row = lambda width: pl.BlockSpec((tm, width), lambda i: (i, 0))
    kern = functools.partial(_proj_xbc_kernel, tiles_per_seq=seq_len // tm)
    return pl.pallas_call(
        kern,
        grid=(t_total // tm,),
        in_specs=[
            row(D_MODEL),
            pl.BlockSpec((hr, D_MODEL), lambda i: (jnp.maximum(i * per - 1, 0), 0)),
            pl.BlockSpec((hr, D_MODEL), lambda i: (jnp.minimum((i + 1) * per, last), 0)),
            _const_spec(w, 1), _const_spec(wdt, 1),
            _const_spec(conv_w8, 1), _const_spec(conv_b, 1), _const_spec(dtb, 1),
            _const_spec(pin, 1), _const_spec(pout, 1),
        ],
        out_specs=[row(D_INNER), row(BC_W), row(LANES)],
        out_shape=[
            jax.ShapeDtypeStruct((t_total, D_INNER), MXU_DTYPE),
            jax.ShapeDtypeStruct((t_total, BC_W), MXU_DTYPE),
            jax.ShapeDtypeStruct((t_total, LANES), jnp.float32),
        ],
        scratch_shapes=[
            pltpu.VMEM((tm + 2 * hr, D_MODEL), MXU_DTYPE),
            pltpu.VMEM((tm + 2 * hr, CONV_CH), jnp.float32),
        ],
        compiler_params=pltpu.CompilerParams(
            dimension_semantics=("parallel",), vmem_limit_bytes=VMEM_LIMIT),
        name="proj_xbc",
    )(h, h, h, w, wdt, conv_w8, conv_b, dtb, pin, pout)


PERM_BLOCK = 256


def _proj_qkv_kernel(h_ref, w_ref, cos_ref, sin_ref, perm_ref, o0_ref, o1_ref, o2_ref, r_ref):
    tm = h_ref.shape[0]
    j = pl.program_id(1)
    acc = jnp.dot(h_ref[...], w_ref[...], preferred_element_type=jnp.float32)
    cos = cos_ref[...]
    sin = sin_ref[...]
    n_cols = QKV_W // LANES
    for c in range(n_cols):
        t = acc[:, c * LANES:(c + 1) * LANES]
        if c < 2 * HEADS_PER_GROUP:
            t = t * cos + pltpu.roll(t, HEAD_DIM // 2, axis=1) * sin
        r_ref[c] = t

    @pl.when(j == 0)
    def _():
        for c in range(n_cols):
            which, hh = divmod(c, HEADS_PER_GROUP)
            o0_ref[0, 0, which, :, hh * LANES:(hh + 1) * LANES] = _mxu(r_ref[c])

    dil1 = ATT_PATTERNS[1][1]

    @pl.when(j == 1)
    def _():
        for r in range(dil1):
            for c in range(n_cols):
                which, hh = divmod(c, HEADS_PER_GROUP)
                o1_ref[0, r, which, :, hh * LANES:(hh + 1) * LANES] = _mxu(
                    r_ref[c, pl.ds(r, tm // dil1, stride=dil1), :])

    dil2 = ATT_PATTERNS[2][1]
    per_class = PERM_BLOCK // dil2

    @pl.when(j == 2)
    def _():
        perm = perm_ref[...]
        for blk in range(tm // PERM_BLOCK):
            rows = slice(blk * PERM_BLOCK, (blk + 1) * PERM_BLOCK)
            tb = jnp.concatenate([_mxu(r_ref[c, rows, :]) for c in range(n_cols)], axis=1)
            pb = _mxu(jnp.dot(perm, tb, preferred_element_type=jnp.float32))
            for r in range(dil2):
                for which in range(3):
                    o2_ref[0, r, which, blk * per_class:(blk + 1) * per_class, :] = (
                        pb[r * per_class:(r + 1) * per_class, which * ATT_OUT:(which + 1) * ATT_OUT])


def _class_permutation(dil):
    p = np.zeros((PERM_BLOCK, PERM_BLOCK), np.float32)
    per_class = PERM_BLOCK // dil
    for r in range(dil):
        for jj in range(per_class):
            p[r * per_class + jj, jj * dil + r] = 1.0
    return jnp.asarray(p, MXU_DTYPE)


def _proj_qkv(h, w, cos2, sin2, batch, seq_len):
    t_total = h.shape[0]
    tm = 1024
    tps = seq_len // tm
    perm = _class_permutation(ATT_PATTERNS[2][1])
    out_specs, out_shape = [], []
    for _, dil in ATT_PATTERNS:
        out_specs.append(pl.BlockSpec((1, dil, 3, tm // dil, ATT_OUT),
                                      lambda i, j: (i // tps, 0, 0, i % tps, 0)))
        out_shape.append(jax.ShapeDtypeStruct((batch, dil, 3, seq_len // dil, ATT_OUT), MXU_DTYPE))
    return pl.pallas_call(
        _proj_qkv_kernel,
        grid=(t_total // tm, ATT_GROUPS),
        in_specs=[
            pl.BlockSpec((tm, D_MODEL), lambda i, j: (i, 0)),
            pl.BlockSpec((D_MODEL, QKV_W), lambda i, j: (0, j)),
            pl.BlockSpec((tm, HEAD_DIM), lambda i, j: (i % tps, 0)),
            pl.BlockSpec((tm, HEAD_DIM), lambda i, j: (i % tps, 0)),
            _const_spec(perm, 2),
        ],
        out_specs=out_specs,
        out_shape=out_shape,
        scratch_shapes=[pltpu.VMEM((QKV_W // LANES, tm, LANES), jnp.float32)],
        compiler_params=pltpu.CompilerParams(
            dimension_semantics=("parallel", "arbitrary"), vmem_limit_bytes=VMEM_LIMIT),
        name="proj_qkv",
    )(h, w, cos2, sin2, perm)


def _attn_kernel(q_ref, kc_ref, kp_ref, kn_ref, vc_ref, vp_ref, vn_ref, o_ref, lse_ref,
                 kbuf, vbuf, *, n, tq, radius):
    i = pl.program_id(1)
    for buf, p_ref, c_ref, n_ref in ((kbuf, kp_ref, kc_ref, kn_ref), (vbuf, vp_ref, vc_ref, vn_ref)):
        buf[0:radius, :] = p_ref[0, 0]
        buf[radius:radius + tq, :] = c_ref[0, 0]
        buf[radius + tq:, :] = n_ref[0, 0]
    sub = ATT_SUB
    win = sub + 2 * radius
    row = lax.broadcasted_iota(jnp.int32, (sub, win), 0)
    col = lax.broadcasted_iota(jnp.int32, (sub, win), 1)
    in_band = jnp.abs(col - row - radius) <= radius
    lane = lax.broadcasted_iota(jnp.int32, (sub, LANES), 1)

    def body(t, carry):
        r0 = pl.multiple_of(t * sub, sub)
        kpos = i * tq + r0 - radius + col
        in_seq = kpos.astype(jnp.uint32) < jnp.uint32(n)
        bias = jnp.where(in_band, jnp.where(in_seq, 0.0, -jnp.inf), -jnp.inf)
        lse_tile = jnp.zeros((sub, LANES), jnp.float32)
        for h in range(HEADS_PER_GROUP):
            hs = slice(h * HEAD_DIM, (h + 1) * HEAD_DIM)
            q = q_ref[0, 0, pl.ds(r0, sub), hs]
            kw = kbuf[pl.ds(r0, win), hs]
            vw = vbuf[pl.ds(r0, win), hs]
            s = lax.dot_general(q, kw, (((1,), (1,)), ((), ())), preferred_element_type=jnp.float32)
            s = s * (1.0 / math.sqrt(HEAD_DIM)) + bias
            m = jnp.max(s, axis=-1, keepdims=True)
            e = jnp.exp(s - m)
            den = jnp.sum(e, axis=-1, keepdims=True)
            o = jnp.dot(_mxu(e), vw, preferred_element_type=jnp.float32) / den
            o_ref[0, pl.ds(r0, sub), hs] = o.astype(o_ref.dtype)
            lse_tile = jnp.where(lane == h, m + jnp.log(den), lse_tile)
        lse_ref[0, pl.ds(r0, sub), :] = lse_tile
        return carry

    lax.fori_loop(0, tq // sub, body, 0, unroll=min(4, tq // sub))


def _attn_group(qkv, radius):
    s_total, _, n, _ = qkv.shape
    tq = min(1024, n)
    per = tq // radius
    last = n // radius - 1
    kern = functools.partial(_attn_kernel, n=n, tq=tq, radius=radius)
    cur = lambda which: pl.BlockSpec((1, 1, tq, ATT_OUT), lambda s, i: (s, which, i, 0))
    prev = lambda which: pl.BlockSpec(
        (1, 1, radius, ATT_OUT), lambda s, i: (s, which, jnp.maximum(i * per - 1, 0), 0))
    nxt = lambda which: pl.BlockSpec(
        (1, 1, radius, ATT_OUT), lambda s, i: (s, which, jnp.minimum((i + 1) * per, last), 0))
    return pl.pallas_call(
        kern,
        grid=(s_total, n // tq),
        in_specs=[cur(0), cur(1), prev(1), nxt(1), cur(2), prev(2), nxt(2)],
        out_specs=[
            pl.BlockSpec((1, tq, ATT_OUT), lambda s, i: (s, i, 0)),
            pl.BlockSpec((1, tq, LANES), lambda s, i: (s, i, 0)),
        ],
        out_shape=[
            jax.ShapeDtypeStruct((s_total, n, ATT_OUT), MXU_DTYPE),
            jax.ShapeDtypeStruct((s_total, n, LANES), jnp.float32),
        ],
        scratch_shapes=[pltpu.VMEM((tq + 2 * radius, ATT_OUT), MXU_DTYPE),
                        pltpu.VMEM((tq + 2 * radius, ATT_OUT), MXU_DTYPE)],
        compiler_params=pltpu.CompilerParams(
            dimension_semantics=("parallel", "arbitrary"), vmem_limit_bytes=VMEM_LIMIT),
        name="band_attn",
    )(qkv, qkv, qkv, qkv, qkv, qkv, qkv)


def _ssd_kernel(*refs, rev):
    if rev:
        xs_ref, bc_ref, dt_ref, sz_ref, yf_ref, alog_ref, e_ref, nw_ref, o_ref, st_ref = refs
    else:
        xs_ref, bc_ref, dt_ref, alog_ref, e_ref, dskip_ref, o_ref, st_ref = refs
    q = SSD_CHUNK

    @pl.when(pl.program_id(1) == 0)
    def _():
        st_ref[...] = jnp.zeros_like(st_ref)

    row = lax.broadcasted_iota(jnp.int32, (q, q), 0)
    col = lax.broadcasted_iota(jnp.int32, (q, q), 1)
    inside = (col >= row) if rev else (col <= row)
    tri = jnp.where(inside, 1.0, 0.0).astype(MXU_DTYPE)
    lane = lax.broadcasted_iota(jnp.int32, (q, LANES), 1)
    lane_off = SSM_HEADS if rev else 0
    e_mat = e_ref[...]
    neg_a = jnp.exp(alog_ref[...])
    expand = lambda f: jnp.dot(_mxu(f), e_mat, preferred_element_type=jnp.float32)
    chunks = range(SSD_STEP // q)
    gsls = [slice(g * GROUP_W, (g + 1) * GROUP_W) for g in range(SSM_GROUPS)]

    for ci in (reversed(chunks) if rev else chunks):
        rows = slice(ci * q, (ci + 1) * q)
        b_gs = [bc_ref[rows, g * D_STATE:(g + 1) * D_STATE] for g in range(SSM_GROUPS)]
        c_gs = [bc_ref[rows, GROUP_W + g * D_STATE:GROUP_W + (g + 1) * D_STATE]
                for g in range(SSM_GROUPS)]
        cbs = [jnp.where(inside, lax.dot_general(c_gs[g], b_gs[g], (((1,), (1,)), ((), ())),
                                                 preferred_element_type=jnp.float32), 0.0)
               for g in range(SSM_GROUPS)]
        y_offs = [jnp.dot(c_gs[g], _mxu(st_ref[:, gsls[g]]), preferred_element_type=jnp.float32)
                  for g in range(SSM_GROUPS)]
        dt = dt_ref[rows, :]
        acum = _split_dot(tri, dt * (-neg_a))
        tot = acum[0:1, :] if rev else acum[q - 1:q, :]
        acum2 = acum * LOG2_E
        acum2_t = acum2.T
        dt_t = dt.T
        e_acc = jnp.exp(acum)
        ex_acc = expand(e_acc)
        ex_w = expand(jnp.exp(tot - acum) * dt)
        edge = _split_dot_r(e_acc[0:SUBLANES, :] if rev else e_acc[q - SUBLANES:, :], e_mat)
        dec_row = edge[0:1, :] if rev else edge[SUBLANES - 1:, :]

        for g in range(SSM_GROUPS):
            gsl = gsls[g]
            x_g = xs_ref[rows, gsl].astype(jnp.float32)
            parts = []
            for jj in range(HEADS_PER_SSM_GROUP // 2):
                ms = []
                for h in (g * HEADS_PER_SSM_GROUP + 2 * jj, g * HEADS_PER_SSM_GROUP + 2 * jj + 1):
                    hl = lane_off + h
                    seg2 = jnp.minimum(acum2[:, hl:hl + 1] - acum2_t[hl:hl + 1, :], 0.0)
                    ms.append(cbs[g] * jnp.exp2(seg2) * dt_t[hl:hl + 1, :])
                m_pair = _mxu(jnp.concatenate(ms, axis=1))
                xp = x_g[:, jj * LANES:(jj + 1) * LANES]
                r_pair = _mxu(jnp.concatenate(
                    [jnp.where(lane < SSM_HEAD_DIM, xp, 0.0), jnp.where(lane >= SSM_HEAD_DIM, xp, 0.0)],
                    axis=0))
                parts.append(jnp.dot(m_pair, r_pair, preferred_element_type=jnp.float32))
            y_g = y_offs[g] * ex_acc[:, gsl] + jnp.concatenate(parts, axis=1)
            b_t = _mxu(b_gs[g].astype(jnp.float32).T)
            s_new = jnp.dot(b_t, _mxu(x_g * ex_w[:, gsl]), preferred_element_type=jnp.float32)
            st_ref[:, gsl] = st_ref[:, gsl] * dec_row[:, gsl] + s_new
            if rev:
                y = (yf_ref[rows, gsl] + y_g) * sz_ref[rows, gsl].astype(jnp.float32)
                ms_ = jnp.mean(y * y, axis=-1, keepdims=True)
                o_ref[rows, gsl] = (y * lax.rsqrt(ms_ + NORM_EPS) * nw_ref[:, gsl]).astype(o_ref.dtype)
            else:
                o_ref[rows, gsl] = y_g + x_g * dskip_ref[:, gsl]


def _ssd_sweep(xs, bc, dt, zg, yf, alog, e_mat, row_w, batch, seq_len, rev):
    t_total = batch * seq_len
    q = SSD_STEP
    nc = seq_len // q
    rowmap = (lambda b, c: b * nc + nc - 1 - c) if rev else (lambda b, c: b * nc + c)
    row = lambda width: pl.BlockSpec((q, width), lambda b, c: (rowmap(b, c), 0))
    specs = [row(D_INNER), row(BC_W), row(LANES)]
    args = [xs, bc, dt]
    if rev:
        specs += [row(D_INNER), row(D_INNER)]
        args += [zg, yf]
    specs += [_const_spec(alog, 2), _const_spec(e_mat, 2), _const_spec(row_w, 2)]
    args += [alog, e_mat, row_w]
    return pl.pallas_call(
        functools.partial(_ssd_kernel, rev=rev),
        grid=(batch, nc),
        in_specs=specs,
        out_specs=row(D_INNER),
        out_shape=jax.ShapeDtypeStruct((t_total, D_INNER), MXU_DTYPE if rev else jnp.float32),
        scratch_shapes=[pltpu.VMEM((D_STATE, D_INNER), jnp.float32)],
        compiler_params=pltpu.CompilerParams(
            dimension_semantics=("parallel", "arbitrary"), vmem_limit_bytes=VMEM_LIMIT),
        name="ssd_bwd" if rev else "ssd_fwd",
    )(*args)


def _mix_kernel(x_ref, a0_ref, a1_ref, a2_ref, l0_ref, l1_ref, l2_ref, ssm_ref, sg_ref,
                wao_ref, wso_ref, wout_ref, o_ref, as_ref, ls_ref):
    tm = x_ref.shape[0]
    ps = jnp.dot(ssm_ref[...], wso_ref[...], preferred_element_type=jnp.float32)
    for gi, (a_ref, l_ref) in enumerate(((a1_ref, l1_ref), (a2_ref, l2_ref))):
        dil = ATT_PATTERNS[gi + 1][1]
        for r in range(dil):
            rows = pl.ds(r, tm // dil, stride=dil)
            a = a_ref[0, r].astype(jnp.float32)
            for h in range(HEADS_PER_GROUP):
                as_ref[gi, h, rows, :] = a[:, h * HEAD_DIM:(h + 1) * HEAD_DIM]
            ls_ref[gi, rows, :] = l_ref[0, r]
    l0, l1, l2 = l0_ref[0, 0], ls_ref[0], ls_ref[1]
    m = jnp.maximum(jnp.maximum(l0, l1), l2)
    e0, e1, e2 = jnp.exp(l0 - m), jnp.exp(l1 - m), jnp.exp(l2 - m)
    den = e0 + e1 + e2
    al0, al1, al2 = e0 / den, e1 / den, e2 / den
    heads = []
    for h in range(HEADS_PER_GROUP):
        hs = slice(h * HEAD_DIM, (h + 1) * HEAD_DIM)
        heads.append(al0[:, h:h + 1] * a0_ref[0, 0, :, hs].astype(jnp.float32)
                     + al1[:, h:h + 1] * as_ref[0, h] + al2[:, h:h + 1] * as_ref[1, h])
    att = jnp.concatenate(heads, axis=1)
    pa = jnp.dot(_mxu(att), wao_ref[...], preferred_element_type=jnp.float32)
    mixed = (sg_ref[:, :D_MODEL].astype(jnp.float32) * pa
             + sg_ref[:, D_MODEL:].astype(jnp.float32) * ps)
    o_ref[...] = x_ref[...] + jnp.dot(_mxu(mixed), wout_ref[...], preferred_element_type=jnp.float32)


def _mix(xf, att, lse, ssm, zg, wao, wso, wout, batch, seq_len):
    t_total = xf.shape[0]
    tm = 512
    tps = seq_len // tm
    row = lambda w, blk=0: pl.BlockSpec((tm, w), lambda i: (i, blk))
    cls = lambda dil, w: pl.BlockSpec((1, dil, tm // dil, w), lambda i: (i // tps, 0, i % tps, 0))
    dils = [d for _, d in ATT_PATTERNS]
    att4 = [a.reshape(batch, d, seq_len // d, ATT_OUT) for a, d in zip(att, dils)]
    lse4 = [l.reshape(batch, d, seq_len // d, LANES) for l, d in zip(lse, dils)]
    return pl.pallas_call(
        _mix_kernel,
        grid=(t_total // tm,),
        in_specs=[row(D_MODEL)] + [cls(d, ATT_OUT) for d in dils] + [cls(d, LANES) for d in dils]
        + [row(D_INNER), row(D_INNER, 1), _const_spec(wao, 1), _const_spec(wso, 1),
           _const_spec(wout, 1)],
        out_specs=row(D_MODEL),
        out_shape=jax.ShapeDtypeStruct((t_total, D_MODEL), jnp.float32),
        scratch_shapes=[pltpu.VMEM((ATT_GROUPS - 1, HEADS_PER_GROUP, tm, HEAD_DIM), jnp.float32),
                        pltpu.VMEM((ATT_GROUPS - 1, tm, LANES), jnp.float32)],
        compiler_params=pltpu.CompilerParams(
            dimension_semantics=("parallel",), vmem_limit_bytes=VMEM_LIMIT),
        name="mix_out",
    )(xf, *att4, *lse4, ssm, zg, wao, wso, wout)


FFN_CHUNK = 1408


def _ffn_kernel(x_ref, nw_ref, wgu_ref, wd_ref, fw_ref, o_ref, *, final_norm):
    x = x_ref[...]
    h = _rms_normed(x, nw_ref[...])
    acc = x
    for c0 in range(0, D_FF, FFN_CHUNK):
        gate = jnp.dot(h, wgu_ref[:, c0:c0 + FFN_CHUNK], preferred_element_type=jnp.float32)
        up = jnp.dot(h, wgu_ref[:, D_FF + c0:D_FF + c0 + FFN_CHUNK],
                     preferred_element_type=jnp.float32)
        a = _mxu(gate * _sigmoid(gate) * up)
        acc = acc + jnp.dot(a, wd_ref[c0:c0 + FFN_CHUNK, :], preferred_element_type=jnp.float32)
    if final_norm:
        ms2 = jnp.mean(acc * acc, axis=-1, keepdims=True)
        acc = acc * lax.rsqrt(ms2 + NORM_EPS) * fw_ref[...]
    o_ref[...] = acc


def _ffn(x1, norm_w, wgu, wd, final_w, final_norm):
    t_total = x1.shape[0]
    tm = 512
    row = pl.BlockSpec((tm, D_MODEL), lambda i: (i, 0))
    return pl.pallas_call(
        functools.partial(_ffn_kernel, final_norm=final_norm),
        grid=(t_total // tm,),
        in_specs=[row, _const_spec(norm_w, 1), _const_spec(wgu, 1), _const_spec(wd, 1),
                  _const_spec(final_w, 1)],
        out_specs=row,
        out_shape=jax.ShapeDtypeStruct((t_total, D_MODEL), jnp.float32),
        compiler_params=pltpu.CompilerParams(
            dimension_semantics=("parallel",), vmem_limit_bytes=VMEM_LIMIT),
        name="ffn",
    )(x1, norm_w, wgu, wd, final_w)


def _pad_lanes(a, width):
    return jnp.pad(a, [(0, 0)] * (a.ndim - 1) + [(0, width - a.shape[-1])])


def _rope_tables(seq_len):
    half = HEAD_DIM // 2
    inv = ROPE_THETA ** (-jnp.arange(half, dtype=jnp.float32) / half)
    ang = jnp.arange(seq_len).astype(jnp.float32)[:, None] * inv[None, :]
    cos, sin = jnp.cos(ang), jnp.sin(ang)
    return jnp.concatenate([cos, cos], axis=1), jnp.concatenate([-sin, sin], axis=1)


def _expand_matrix(row0):
    r = np.zeros((LANES, D_INNER), np.float32)
    for h in range(SSM_HEADS):
        r[row0 + h, h * SSM_HEAD_DIM:(h + 1) * SSM_HEAD_DIM] = 1.0
    return jnp.asarray(r, MXU_DTYPE)


def _prep_params(norm_mix, w_in, conv_w, conv_b, a_log, dt_bias, d_skip, ssm_norm,
                 w_attn_out, w_ssm_out, w_out, norm_ffn, w_gate_up, w_down, norm_final):
    b = [int(v) for v in np.cumsum([ATT_W, ATT_W, ATT_W, D_INNER, CONV_CH, 2 * SSM_HEADS, D_MODEL])]
    q, k, v = w_in[..., :b[0]], w_in[..., b[0]:b[1]], w_in[..., b[1]:b[2]]
    z, xbc, dt = w_in[..., b[2]:b[3]], w_in[..., b[3]:b[4]], w_in[..., b[4]:b[5]]
    ga, gm = w_in[..., b[5]:b[6]], w_in[..., b[6]:]
    grp = lambda t, g: t[..., g * ATT_OUT:(g + 1) * ATT_OUT]
    w_qkv = jnp.concatenate([grp(t, g) for g in range(ATT_GROUPS) for t in (q, k, v)], axis=-1)
    return {
        "norm_mix": norm_mix[:, None, :],
        "w_gates": _mxu(jnp.concatenate([z, ga, gm], axis=-1)),
        "w_xbc": _mxu(xbc),
        "w_dt": _mxu(_pad_lanes(dt, LANES)),
        "w_qkv": _mxu(w_qkv),
        "conv_w": jnp.pad(conv_w, ((0, 0), (0, SUBLANES - CONV_K), (0, 0))),
        "conv_b": conv_b[:, None, :],
        "a_log": _pad_lanes(a_log.reshape(DEPTH, 1, 2 * SSM_HEADS), LANES),
        "dt_bias": _pad_lanes(dt_bias.reshape(DEPTH, 1, 2 * SSM_HEADS), LANES),
        "d_skip": jnp.repeat(d_skip, SSM_HEAD_DIM, axis=-1)[:, None, :],
        "ssm_norm": ssm_norm[:, None, :],
        "w_attn_out": _mxu(w_attn_out),
        "w_ssm_out": _mxu(w_ssm_out),
        "w_out": _mxu(w_out),
        "norm_ffn": norm_ffn[:, None, :],
        "w_gate_up": _mxu(w_gate_up),
        "w_down": _mxu(w_down),
        "norm_final": norm_final[None, :],
        "e_fwd": _expand_matrix(0),
        "e_bwd": _expand_matrix(SSM_HEADS),
    }


def _trunk(x, p):
    batch, seq_len, _ = x.shape
    xf = x.reshape(batch * seq_len, D_MODEL)
    cos2, sin2 = _rope_tables(seq_len)
    for i in range(DEPTH):
        nw = p["norm_mix"][i]
        zg, h = _proj_gates(xf, nw, p["w_gates"][i])
        xs, bc, dt = _proj_xbc(h, p["w_xbc"][i], p["w_dt"][i], p["conv_w"][i], p["conv_b"][i],
                               p["dt_bias"][i], seq_len)
        qkv = _proj_qkv(h, p["w_qkv"][i], cos2, sin2, batch, seq_len)
        att, lse = [], []
        for g, (window, dil) in enumerate(ATT_PATTERNS):
            n = seq_len // dil
            o, l = _attn_group(qkv[g].reshape(batch * dil, 3, n, ATT_OUT), window // (2 * dil))
            att.append(o)
            lse.append(l)
        yf = _ssd_sweep(xs, bc, dt, None, None, p["a_log"][i], p["e_fwd"], p["d_skip"][i],
                        batch, seq_len, False)
        ssm = _ssd_sweep(xs, bc, dt, zg, yf, p["a_log"][i], p["e_bwd"], p["ssm_norm"][i],
                         batch, seq_len, True)
        x1 = _mix(xf, att, lse, ssm, zg, p["w_attn_out"][i], p["w_ssm_out"][i], p["w_out"][i],
                  batch, seq_len)
        xf = _ffn(x1, p["norm_ffn"][i], p["w_gate_up"][i], p["w_down"][i], p["norm_final"],
                  i == DEPTH - 1)
    return xf.reshape(batch, seq_len, D_MODEL)


def kernel(x_prompt, x_sample, norm_mix, w_in, conv_w, conv_b, a_log, dt_bias, d_skip, ssm_norm,
           w_attn_out, w_ssm_out, w_out, norm_ffn, w_gate_up, w_down, norm_final):
    p = _prep_params(norm_mix, w_in, conv_w, conv_b, a_log, dt_bias, d_skip, ssm_norm,
                     w_attn_out, w_ssm_out, w_out, norm_ffn, w_gate_up, w_down, norm_final)
    return (_trunk(x_prompt, p), _trunk(x_sample, p))
```

```python
import functools
import math

import jax
import jax.numpy as jnp
import numpy as np
from jax import lax
from jax.experimental import pallas as pl
from jax.experimental.pallas import tpu as pltpu

D_MODEL = 1024
DEPTH = 4
HEAD_DIM = 128
ATT_PATTERNS = ((128, 1), (512, 4), (2048, 16))
HEADS_PER_GROUP = 4
ATT_GROUPS = len(ATT_PATTERNS)
ATT_W = HEADS_PER_GROUP * ATT_GROUPS * HEAD_DIM
ATT_OUT = HEADS_PER_GROUP * HEAD_DIM
ROPE_THETA = 10000.0
D_INNER = 2048
SSM_HEAD_DIM = 64
SSM_HEADS = D_INNER // SSM_HEAD_DIM
SSM_GROUPS = 4
D_STATE = 128
CONV_K = 5
CONV_PAD = CONV_K // 2
BC_W = 2 * SSM_GROUPS * D_STATE
CONV_CH = D_INNER + BC_W
D_FF = 2816
NORM_EPS = 1e-6

MXU_DTYPE = jnp.bfloat16
LANES = 128
SUBLANES = 8
PACKED_ROWS = 16
SSD_CHUNK = 128
SSD_STEP = 512
LOG2_E = 1.4426950408889634
GROUP_W = D_INNER // SSM_GROUPS
HEADS_PER_SSM_GROUP = SSM_HEADS // SSM_GROUPS
ATT_SUB = 128
QKV_W = 3 * ATT_OUT

VMEM_LIMIT = 56 * 1024 * 1024


def _mxu(x):
    return x.astype(MXU_DTYPE)


def _sigmoid(x):
    return 1.0 / (1.0 + jnp.exp(-x))


def _split3(x):
    hi = _mxu(x)
    r1 = x - hi.astype(jnp.float32)
    mid = _mxu(r1)
    return hi, mid, _mxu(r1 - mid.astype(jnp.float32))


def _split_dot(lhs01, x):
    return sum(jnp.dot(lhs01, p, preferred_element_type=jnp.float32) for p in _split3(x))


def _split_dot_r(x, rhs01):
    return sum(jnp.dot(p, rhs01, preferred_element_type=jnp.float32) for p in _split3(x))


def _rms_normed(x, w):
    ms = jnp.mean(x * x, axis=-1, keepdims=True)
    return _mxu(x * lax.rsqrt(ms + NORM_EPS) * w)


def _const_spec(a, grid_rank):
    zeros = (0,) * a.ndim
    if grid_rank == 1:
        return pl.BlockSpec(a.shape, lambda i: zeros, pipeline_mode=pl.Buffered(1))
    return pl.BlockSpec(a.shape, lambda i, j: zeros, pipeline_mode=pl.Buffered(1))


GATES_CHUNK = 1024


def _proj_gates_kernel(x_ref, nw_ref, w_ref, o_ref, h_ref):
    h = _rms_normed(x_ref[...], nw_ref[...])
    h_ref[...] = h
    for c0 in range(0, w_ref.shape[1], GATES_CHUNK):
        cols = slice(c0, c0 + GATES_CHUNK)
        acc = jnp.dot(h, w_ref[:, cols], preferred_element_type=jnp.float32)
        s = _sigmoid(acc)
        o_ref[:, cols] = (acc * s if c0 < D_INNER else s).astype(o_ref.dtype)


def _proj_gates(xf, norm_w, w):
    t_total = xf.shape[0]
    tm = 1024
    row = lambda width: pl.BlockSpec((tm, width), lambda i: (i, 0))
    return pl.pallas_call(
        _proj_gates_kernel,
        grid=(t_total // tm,),
        in_specs=[row(D_MODEL), _const_spec(norm_w, 1), _const_spec(w, 1)],
        out_specs=[row(w.shape[1]), row(D_MODEL)],
        out_shape=[jax.ShapeDtypeStruct((t_total, w.shape[1]), MXU_DTYPE),
                   jax.ShapeDtypeStruct((t_total, D_MODEL), MXU_DTYPE)],
        compiler_params=pltpu.CompilerParams(
            dimension_semantics=("parallel",), vmem_limit_bytes=VMEM_LIMIT),
        name="proj_gates",
    )(xf, norm_w, w)


CONV_BLOCK = 256
CONV_SEG = CONV_BLOCK // SUBLANES


def _proj_xbc_kernel(h_ref, hp_ref, hn_ref, w_ref, wdt_ref, cw_ref, cbias_ref, dtb_ref, pin_ref,
                     pout_ref, xs_ref, bc_ref, dt_ref, hh_ref, acc_ref, *, tiles_per_seq):
    tm = h_ref.shape[0]
    hr = PACKED_ROWS
    pb = CONV_BLOCK
    nb = tm // pb
    iseq = pl.program_id(0) % tiles_per_seq
    pin = pin_ref[...]
    for b in range(nb):
        blk = slice(b * pb, (b + 1) * pb)
        hh_ref[blk, :] = _mxu(jnp.dot(pin, h_ref[blk, :], preferred_element_type=jnp.float32))
    hh_ref[tm:tm + hr, :] = hp_ref[...]
    hh_ref[tm + hr:, :] = hn_ref[...]
    acc_ref[...] = jnp.dot(hh_ref[...], w_ref[...], preferred_element_type=jnp.float32)
    pout = pout_ref[...]
    s8 = SUBLANES
    for cc in range(CONV_CH // GROUP_W):
        sl = slice(cc * GROUP_W, (cc + 1) * GROUP_W)
        sub = lax.broadcasted_iota(jnp.int32, (s8, GROUP_W), 0)
        before = jnp.where(iseq > 0, acc_ref[tm + hr - s8:tm + hr, sl], 0.0)
        after = jnp.where(iseq < tiles_per_seq - 1, acc_ref[tm + hr:tm + hr + s8, sl], 0.0)
        blocks = [acc_ref[b * pb:(b + 1) * pb, sl] for b in range(nb)]
        down = lambda v, n=1: pltpu.roll(v, n, axis=0)
        up = lambda v, n=1: pltpu.roll(v, s8 - n, axis=0)
        for b in range(nb):
            cur = blocks[b]
            if b == 0:
                fill_m1, fill_m2 = down(before, 1), down(before, 2)
            else:
                fill_m1, fill_m2 = down(blocks[b - 1][pb - s8:]), down(blocks[b - 1][pb - 2 * s8:pb - s8])
            if b == nb - 1:
                fill_p0, fill_p1 = up(after, 1), up(after, 2)
            else:
                fill_p0, fill_p1 = up(blocks[b + 1][:s8]), up(blocks[b + 1][s8:2 * s8])
            e_m1 = jnp.where(sub == 0, fill_m1, down(cur[pb - s8:]))
            e_m2 = jnp.where(sub == 0, fill_m2, down(cur[pb - 2 * s8:pb - s8]))
            e_p0 = jnp.where(sub == s8 - 1, fill_p0, up(cur[:s8]))
            e_p1 = jnp.where(sub == s8 - 1, fill_p1, up(cur[s8:2 * s8]))
            ext = jnp.concatenate([e_m2, e_m1, cur, e_p0, e_p1], axis=0)
            a = cbias_ref[:, sl]
            for k in range(CONV_K):
                a = a + cw_ref[k:k + 1, sl] * ext[k * s8:k * s8 + pb]
            y = _mxu(a * _sigmoid(a))
            y = jnp.dot(pout, y, preferred_element_type=jnp.float32).astype(MXU_DTYPE)
            blk = slice(b * pb, (b + 1) * pb)
            if cc < D_INNER // GROUP_W:
                xs_ref[blk, sl] = y
            else:
                bc_ref[blk, cc * GROUP_W - D_INNER:(cc + 1) * GROUP_W - D_INNER] = y
    v = jnp.dot(h_ref[...], wdt_ref[...], preferred_element_type=jnp.float32) + dtb_ref[...]
    dt_ref[...] = jnp.maximum(v, 0.0) + jnp.log1p(jnp.exp(-jnp.abs(v)))


def _conv_permutation():
    p = np.zeros((CONV_BLOCK, CONV_BLOCK), np.float32)
    for t in range(CONV_BLOCK):
        p[(t % CONV_SEG) * SUBLANES + t // CONV_SEG, t] = 1.0
    return p


def _proj_xbc(h, w, wdt, conv_w8, conv_b, dtb, seq_len):
    t_total = h.shape[0]
    tm = 512
    hr = PACKED_ROWS
    per = tm // hr
    last = t_total // hr - 1
    p = _conv_permutation()
    pin, pout = jnp.asarray(p, MXU_DTYPE), jnp.asarray(p.T, MXU_DTYPE)
    row = lambda width: pl.BlockSpec((tm, width), lambda i: (i, 0))
    kern = functools.partial(_proj_xbc_kernel, tiles_per_seq=seq_len // tm)
    return pl.pallas_call(
        kern,
        grid=(t_total // tm,),
        in_specs=[
            row(D_MODEL),
            pl.BlockSpec((hr, D_MODEL), lambda i: (jnp.maximum(i * per - 1, 0), 0)),
            pl.BlockSpec((hr, D_MODEL), lambda i: (jnp.minimum((i + 1) * per, last), 0)),
            _const_spec(w, 1), _const_spec(wdt, 1),
            _const_spec(conv_w8, 1), _const_spec(conv_b, 1), _const_spec(dtb, 1),
            _const_spec(pin, 1), _const_spec(pout, 1),
        ],
        out_specs=[row(D_INNER), row(BC_W), row(LANES)],
        out_shape=[
            jax.ShapeDtypeStruct((t_total, D_INNER), MXU_DTYPE),
            jax.ShapeDtypeStruct((t_total, BC_W), MXU_DTYPE),
            jax.ShapeDtypeStruct((t_total, LANES), jnp.float32),
        ],
        scratch_shapes=[
            pltpu.VMEM((tm + 2 * hr, D_MODEL), MXU_DTYPE),
            pltpu.VMEM((tm + 2 * hr, CONV_CH), jnp.float32),
        ],
        compiler_params=pltpu.CompilerParams(
            dimension_semantics=("parallel",), vmem_limit_bytes=VMEM_LIMIT),
        name="proj_xbc",
    )(h, h, h, w, wdt, conv_w8, conv_b, dtb, pin, pout)


PERM_BLOCK = 256


def _proj_qkv_kernel(h_ref, w_ref, cos_ref, sin_ref, perm_ref, o0_ref, o1_ref, o2_ref, r_ref):
    tm = h_ref.shape[0]
    j = pl.program_id(1)
    n_cols = QKV_W // LANES

    def roped(c, t):
        if c < 2 * HEADS_PER_GROUP:
            return t * cos_ref[...] + pltpu.roll(t, HEAD_DIM // 2, axis=1) * sin_ref[...]
        return t

    def column_pairs():
        for cp in range(n_cols // 2):
            acc = jnp.dot(h_ref[...], w_ref[:, 2 * cp * LANES:2 * (cp + 1) * LANES],
                          preferred_element_type=jnp.float32)
            for c in (2 * cp, 2 * cp + 1):
                yield c, roped(c, acc[:, (c - 2 * cp) * LANES:(c - 2 * cp + 1) * LANES])

    @pl.when(j == 0)
    def _():
        for c, t in column_pairs():
            which, hh = divmod(c, HEADS_PER_GROUP)
            o0_ref[0, 0, which, :, hh * LANES:(hh + 1) * LANES] = _mxu(t)

    dil1 = ATT_PATTERNS[1][1]

    @pl.when(j == 1)
    def _():
        for c, t in column_pairs():
            r_ref[c] = t
            which, hh = divmod(c, HEADS_PER_GROUP)
            for r in range(dil1):
                o1_ref[0, r, which, :, hh * LANES:(hh + 1) * LANES] = _mxu(
                    r_ref[c, pl.ds(r, tm // dil1, stride=dil1), :])

    dil2 = ATT_PATTERNS[2][1]
    per_class = PERM_BLOCK // dil2

    @pl.when(j == 2)
    def _():
        perm = perm_ref[...]
        for c, t in column_pairs():
            r_ref[c] = t
        for blk in range(tm // PERM_BLOCK):
            rows = slice(blk * PERM_BLOCK, (blk + 1) * PERM_BLOCK)
            tb = jnp.concatenate([_mxu(r_ref[c, rows, :]) for c in range(n_cols)], axis=1)
            pb = _mxu(jnp.dot(perm, tb, preferred_element_type=jnp.float32))
            for r in range(dil2):
                for which in range(3):
                    o2_ref[0, r, which, blk * per_class:(blk + 1) * per_class, :] = (
                        pb[r * per_class:(r + 1) * per_class, which * ATT_OUT:(which + 1) * ATT_OUT])


def _class_permutation(dil):
    p = np.zeros((PERM_BLOCK, PERM_BLOCK), np.float32)
    per_class = PERM_BLOCK // dil
    for r in range(dil):
        for jj in range(per_class):
            p[r * per_class + jj, jj * dil + r] = 1.0
    return jnp.asarray(p, MXU_DTYPE)


def _proj_qkv(h, w, cos2, sin2, batch, seq_len):
    t_total = h.shape[0]
    tm = 1024
    tps = seq_len // tm
    perm = _class_permutation(ATT_PATTERNS[2][1])
    out_specs, out_shape = [], []
    for _, dil in ATT_PATTERNS:
        out_specs.append(pl.BlockSpec((1, dil, 3, tm // dil, ATT_OUT),
                                      lambda i, j: (i // tps, 0, 0, i % tps, 0)))
        out_shape.append(jax.ShapeDtypeStruct((batch, dil, 3, seq_len // dil, ATT_OUT), MXU_DTYPE))
    return pl.pallas_call(
        _proj_qkv_kernel,
        grid=(t_total // tm, ATT_GROUPS),
        in_specs=[
            pl.BlockSpec((tm, D_MODEL), lambda i, j: (i, 0)),
            pl.BlockSpec((D_MODEL, QKV_W), lambda i, j: (0, j)),
            pl.BlockSpec((tm, HEAD_DIM), lambda i, j: (i % tps, 0)),
            pl.BlockSpec((tm, HEAD_DIM), lambda i, j: (i % tps, 0)),
            _const_spec(perm, 2),
        ],
        out_specs=out_specs,
        out_shape=out_shape,
        scratch_shapes=[pltpu.VMEM((QKV_W // LANES, tm, LANES), jnp.float32)],
        compiler_params=pltpu.CompilerParams(
            dimension_semantics=("parallel", "arbitrary"), vmem_limit_bytes=VMEM_LIMIT),
        name="proj_qkv",
    )(h, w, cos2, sin2, perm)


def _attn_kernel(q_ref, kc_ref, kp_ref, kn_ref, vc_ref, vp_ref, vn_ref, o_ref, lse_ref,
                 kbuf, vbuf, *, n, tq, radius):
    i = pl.program_id(1)
    for buf, p_ref, c_ref, n_ref in ((kbuf, kp_ref, kc_ref, kn_ref), (vbuf, vp_ref, vc_ref, vn_ref)):
        buf[0:radius, :] = p_ref[0, 0]
        buf[radius:radius + tq, :] = c_ref[0, 0]
        buf[radius + tq:, :] = n_ref[0, 0]
    sub = ATT_SUB
    win = sub + 2 * radius
    row = lax.broadcasted_iota(jnp.int32, (sub, win), 0)
    col = lax.broadcasted_iota(jnp.int32, (sub, win), 1)
    in_band = jnp.abs(col - row - radius) <= radius
    lane = lax.broadcasted_iota(jnp.int32, (sub, LANES), 1)

    def body(t, carry):
        r0 = pl.multiple_of(t * sub, sub)
        kpos = i * tq + r0 - radius + col
        in_seq = kpos.astype(jnp.uint32) < jnp.uint32(n)
        bias = jnp.where(in_band, jnp.where(in_seq, 0.0, -jnp.inf), -jnp.inf)
        lse_tile = jnp.zeros((sub, LANES), jnp.float32)
        for h in range(HEADS_PER_GROUP):
            hs = slice(h * HEAD_DIM, (h + 1) * HEAD_DIM)
            q = q_ref[0, 0, pl.ds(r0, sub), hs]
            kw = kbuf[pl.ds(r0, win), hs]
            vw = vbuf[pl.ds(r0, win), hs]
            s = lax.dot_general(q, kw, (((1,), (1,)), ((), ())), preferred_element_type=jnp.float32)
            s = s * (1.0 / math.sqrt(HEAD_DIM)) + bias
            m = jnp.max(s, axis=-1, keepdims=True)
            e = jnp.exp(s - m)
            den = jnp.sum(e, axis=-1, keepdims=True)
            o = jnp.dot(_mxu(e), vw, preferred_element_type=jnp.float32) / den
            o_ref[0, pl.ds(r0, sub), hs] = o.astype(o_ref.dtype)
            lse_tile = jnp.where(lane == h, m + jnp.log(den), lse_tile)
        lse_ref[0, pl.ds(r0, sub), :] = lse_tile
        return carry

    lax.fori_loop(0, tq // sub, body, 0, unroll=min(4, tq // sub))


def _attn_group(qkv, radius):
    s_total, _, n, _ = qkv.shape
    tq = min(1024, n)
    per = tq // radius
    last = n // radius - 1
    kern = functools.partial(_attn_kernel, n=n, tq=tq, radius=radius)
    cur = lambda which: pl.BlockSpec((1, 1, tq, ATT_OUT), lambda s, i: (s, which, i, 0))
    prev = lambda which: pl.BlockSpec(
        (1, 1, radius, ATT_OUT), lambda s, i: (s, which, jnp.maximum(i * per - 1, 0), 0))
    nxt = lambda which: pl.BlockSpec(
        (1, 1, radius, ATT_OUT), lambda s, i: (s, which, jnp.minimum((i + 1) * per, last), 0))
    return pl.pallas_call(
        kern,
        grid=(s_total, n // tq),
        in_specs=[cur(0), cur(1), prev(1), nxt(1), cur(2), prev(2), nxt(2)],
        out_specs=[
            pl.BlockSpec((1, tq, ATT_OUT), lambda s, i: (s, i, 0)),
            pl.BlockSpec((1, tq, LANES), lambda s, i: (s, i, 0)),
        ],
        out_shape=[
            jax.ShapeDtypeStruct((s_total, n, ATT_OUT), MXU_DTYPE),
            jax.ShapeDtypeStruct((s_total, n, LANES), jnp.float32),
        ],
        scratch_shapes=[pltpu.VMEM((tq + 2 * radius, ATT_OUT), MXU_DTYPE),
                        pltpu.VMEM((tq + 2 * radius, ATT_OUT), MXU_DTYPE)],
        compiler_params=pltpu.CompilerParams(
            dimension_semantics=("parallel", "arbitrary"), vmem_limit_bytes=VMEM_LIMIT),
        name="band_attn",
    )(qkv, qkv, qkv, qkv, qkv, qkv, qkv)


def _ssd_kernel(*refs, rev):
    if rev:
        xs_ref, bc_ref, dt_ref, sz_ref, yf_ref, alog_ref, e_ref, nw_ref, o_ref, st_ref = refs
    else:
        xs_ref, bc_ref, dt_ref, alog_ref, e_ref, dskip_ref, o_ref, st_ref = refs
    q = SSD_CHUNK

    @pl.when(pl.program_id(1) == 0)
    def _():
        st_ref[...] = jnp.zeros_like(st_ref)

    row = lax.broadcasted_iota(jnp.int32, (q, q), 0)
    col = lax.broadcasted_iota(jnp.int32, (q, q), 1)
    inside = (col >= row) if rev else (col <= row)
    tri = jnp.where(inside, 1.0, 0.0).astype(MXU_DTYPE)
    lane = lax.broadcasted_iota(jnp.int32, (q, LANES), 1)
    lane_off = SSM_HEADS if rev else 0
    e_mat = e_ref[...]
    neg_a = jnp.exp(alog_ref[...])
    expand = lambda f: jnp.dot(_mxu(f), e_mat, preferred_element_type=jnp.float32)
    chunks = range(SSD_STEP // q)
    gsls = [slice(g * GROUP_W, (g + 1) * GROUP_W) for g in range(SSM_GROUPS)]

    for ci in (reversed(chunks) if rev else chunks):
        rows = slice(ci * q, (ci + 1) * q)
        b_gs = [bc_ref[rows, g * D_STATE:(g + 1) * D_STATE] for g in range(SSM_GROUPS)]
        c_gs = [bc_ref[rows, GROUP_W + g * D_STATE:GROUP_W + (g + 1) * D_STATE]
                for g in range(SSM_GROUPS)]
        cbs = [jnp.where(inside, lax.dot_general(c_gs[g], b_gs[g], (((1,), (1,)), ((), ())),
                                                 preferred_element_type=jnp.float32), 0.0)
               for g in range(SSM_GROUPS)]
        y_offs = [jnp.dot(c_gs[g], _mxu(st_ref[:, gsls[g]]), preferred_element_type=jnp.float32)
                  for g in range(SSM_GROUPS)]
        dt = dt_ref[rows, :]
        acum = _split_dot(tri, dt * (-neg_a))
        tot = acum[0:1, :] if rev else acum[q - 1:q, :]
        acum2 = acum * LOG2_E
        acum2_t = acum2.T
        dt_t = dt.T
        e_acc = jnp.exp(acum)
        ex_acc = expand(e_acc)
        ex_w = expand(jnp.exp(tot - acum) * dt)
        edge = _split_dot_r(e_acc[0:SUBLANES, :] if rev else e_acc[q - SUBLANES:, :], e_mat)
        dec_row = edge[0:1, :] if rev else edge[SUBLANES - 1:, :]

        for g in range(SSM_GROUPS):
            gsl = gsls[g]
            x_g = xs_ref[rows, gsl].astype(jnp.float32)
            parts = []
            for jj in range(HEADS_PER_SSM_GROUP // 2):
                ms = []
                for h in (g * HEADS_PER_SSM_GROUP + 2 * jj, g * HEADS_PER_SSM_GROUP + 2 * jj + 1):
                    hl = lane_off + h
                    seg2 = jnp.minimum(acum2[:, hl:hl + 1] - acum2_t[hl:hl + 1, :], 0.0)
                    ms.append(cbs[g] * jnp.exp2(seg2) * dt_t[hl:hl + 1, :])
                m_pair = _mxu(jnp.concatenate(ms, axis=1))
                xp = x_g[:, jj * LANES:(jj + 1) * LANES]
                r_pair = _mxu(jnp.concatenate(
                    [jnp.where(lane < SSM_HEAD_DIM, xp, 0.0), jnp.where(lane >= SSM_HEAD_DIM, xp, 0.0)],
                    axis=0))
                parts.append(jnp.dot(m_pair, r_pair, preferred_element_type=jnp.float32))
            y_g = y_offs[g] * ex_acc[:, gsl] + jnp.concatenate(parts, axis=1)
            b_t = _mxu(b_gs[g].astype(jnp.float32).T)
            s_new = jnp.dot(b_t, _mxu(x_g * ex_w[:, gsl]), preferred_element_type=jnp.float32)
            st_ref[:, gsl] = st_ref[:, gsl] * dec_row[:, gsl] + s_new
            if rev:
                y = (yf_ref[rows, gsl] + y_g) * sz_ref[rows, gsl].astype(jnp.float32)
                ms_ = jnp.mean(y * y, axis=-1, keepdims=True)
                o_ref[rows, gsl] = (y * lax.rsqrt(ms_ + NORM_EPS) * nw_ref[:, gsl]).astype(o_ref.dtype)
            else:
                o_ref[rows, gsl] = y_g + x_g * dskip_ref[:, gsl]


def _ssd_sweep(xs, bc, dt, zg, yf, alog, e_mat, row_w, batch, seq_len, rev):
    t_total = batch * seq_len
    q = SSD_STEP
    nc = seq_len // q
    rowmap = (lambda b, c: b * nc + nc - 1 - c) if rev else (lambda b, c: b * nc + c)
    row = lambda width: pl.BlockSpec((q, width), lambda b, c: (rowmap(b, c), 0))
    specs = [row(D_INNER), row(BC_W), row(LANES)]
    args = [xs, bc, dt]
    if rev:
        specs += [row(D_INNER), row(D_INNER)]
        args += [zg, yf]
    specs += [_const_spec(alog, 2), _const_spec(e_mat, 2), _const_spec(row_w, 2)]
    args += [alog, e_mat, row_w]
    return pl.pallas_call(
        functools.partial(_ssd_kernel, rev=rev),
        grid=(batch, nc),
        in_specs=specs,
        out_specs=row(D_INNER),
        out_shape=jax.ShapeDtypeStruct((t_total, D_INNER), MXU_DTYPE if rev else jnp.float32),
        scratch_shapes=[pltpu.VMEM((D_STATE, D_INNER), jnp.float32)],
        compiler_params=pltpu.CompilerParams(
            dimension_semantics=("parallel", "arbitrary"), vmem_limit_bytes=VMEM_LIMIT),
        name="ssd_bwd" if rev else "ssd_fwd",
    )(*args)


def _mix_kernel(x_ref, a0_ref, a1_ref, a2_ref, l0_ref, l1_ref, l2_ref, ssm_ref, sg_ref,
                wao_ref, wso_ref, wout_ref, o_ref, as_ref, ls_ref):
    tm = x_ref.shape[0]
    ps = jnp.dot(ssm_ref[...], wso_ref[...], preferred_element_type=jnp.float32)
    for gi, (a_ref, l_ref) in enumerate(((a1_ref, l1_ref), (a2_ref, l2_ref))):
        dil = ATT_PATTERNS[gi + 1][1]
        for r in range(dil):
            rows = pl.ds(r, tm // dil, stride=dil)
            a = a_ref[0, r].astype(jnp.float32)
            for h in range(HEADS_PER_GROUP):
                as_ref[gi, h, rows, :] = a[:, h * HEAD_DIM:(h + 1) * HEAD_DIM]
            ls_ref[gi, rows, :] = l_ref[0, r]
    l0, l1, l2 = l0_ref[0, 0], ls_ref[0], ls_ref[1]
    m = jnp.maximum(jnp.maximum(l0, l1), l2)
    e0, e1, e2 = jnp.exp(l0 - m), jnp.exp(l1 - m), jnp.exp(l2 - m)
    den = e0 + e1 + e2
    al0, al1, al2 = e0 / den, e1 / den, e2 / den
    heads = []
    for h in range(HEADS_PER_GROUP):
        hs = slice(h * HEAD_DIM, (h + 1) * HEAD_DIM)
        heads.append(al0[:, h:h + 1] * a0_ref[0, 0, :, hs].astype(jnp.float32)
                     + al1[:, h:h + 1] * as_ref[0, h] + al2[:, h:h + 1] * as_ref[1, h])
    att = jnp.concatenate(heads, axis=1)
    pa = jnp.dot(_mxu(att), wao_ref[...], preferred_element_type=jnp.float32)
    mixed = (sg_ref[:, :D_MODEL].astype(jnp.float32) * pa
             + sg_ref[:, D_MODEL:].astype(jnp.float32) * ps)
    o_ref[...] = x_ref[...] + jnp.dot(_mxu(mixed), wout_ref[...], preferred_element_type=jnp.float32)


def _mix(xf, att, lse, ssm, zg, wao, wso, wout, batch, seq_len):
    t_total = xf.shape[0]
    tm = 512
    tps = seq_len // tm
    row = lambda w, blk=0: pl.BlockSpec((tm, w), lambda i: (i, blk))
    cls = lambda dil, w: pl.BlockSpec((1, dil, tm // dil, w), lambda i: (i // tps, 0, i % tps, 0))
    dils = [d for _, d in ATT_PATTERNS]
    att4 = [a.reshape(batch, d, seq_len // d, ATT_OUT) for a, d in zip(att, dils)]
    lse4 = [l.reshape(batch, d, seq_len // d, LANES) for l, d in zip(lse, dils)]
    return pl.pallas_call(
        _mix_kernel,
        grid=(t_total // tm,),
        in_specs=[row(D_MODEL)] + [cls(d, ATT_OUT) for d in dils] + [cls(d, LANES) for d in dils]
        + [row(D_INNER), row(D_INNER, 1), _const_spec(wao, 1), _const_spec(wso, 1),
           _const_spec(wout, 1)],
        out_specs=row(D_MODEL),
        out_shape=jax.ShapeDtypeStruct((t_total, D_MODEL), jnp.float32),
        scratch_shapes=[pltpu.VMEM((ATT_GROUPS - 1, HEADS_PER_GROUP, tm, HEAD_DIM), jnp.float32),
                        pltpu.VMEM((ATT_GROUPS - 1, tm, LANES), jnp.float32)],
        compiler_params=pltpu.CompilerParams(
            dimension_semantics=("parallel",), vmem_limit_bytes=VMEM_LIMIT),
        name="mix_out",
    )(xf, *att4, *lse4, ssm, zg, wao, wso, wout)


FFN_CHUNK = 1408


def _ffn_kernel(x_ref, nw_ref, wgu_ref, wd_ref, fw_ref, o_ref, *, final_norm):
    x = x_ref[...]
    h = _rms_normed(x, nw_ref[...])
    acc = x
    for c0 in range(0, D_FF, FFN_CHUNK):
        gate = jnp.dot(h, wgu_ref[:, c0:c0 + FFN_CHUNK], preferred_element_type=jnp.float32)
        up = jnp.dot(h, wgu_ref[:, D_FF + c0:D_FF + c0 + FFN_CHUNK],
                     preferred_element_type=jnp.float32)
        a = _mxu(gate * _sigmoid(gate) * up)
        acc = acc + jnp.dot(a, wd_ref[c0:c0 + FFN_CHUNK, :], preferred_element_type=jnp.float32)
    if final_norm:
        ms2 = jnp.mean(acc * acc, axis=-1, keepdims=True)
        acc = acc * lax.rsqrt(ms2 + NORM_EPS) * fw_ref[...]
    o_ref[...] = acc


def _ffn(x1, norm_w, wgu, wd, final_w, final_norm):
    t_total = x1.shape[0]
    tm = 512
    row = pl.BlockSpec((tm, D_MODEL), lambda i: (i, 0))
    return pl.pallas_call(
        functools.partial(_ffn_kernel, final_norm=final_norm),
        grid=(t_total // tm,),
        in_specs=[row, _const_spec(norm_w, 1), _const_spec(wgu, 1), _const_spec(wd, 1),
                  _const_spec(final_w, 1)],
        out_specs=row,
        out_shape=jax.ShapeDtypeStruct((t_total, D_MODEL), jnp.float32),
        compiler_params=pltpu.CompilerParams(
            dimension_semantics=("parallel",), vmem_limit_bytes=VMEM_LIMIT),
        name="ffn",
    )(x1, norm_w, wgu, wd, final_w)


def _pad_lanes(a, width):
    return jnp.pad(a, [(0, 0)] * (a.ndim - 1) + [(0, width - a.shape[-1])])


def _rope_tables(seq_len):
    half = HEAD_DIM // 2
    inv = ROPE_THETA ** (-jnp.arange(half, dtype=jnp.float32) / half)
    ang = jnp.arange(seq_len).astype(jnp.float32)[:, None] * inv[None, :]
    cos, sin = jnp.cos(ang), jnp.sin(ang)
    return jnp.concatenate([cos, cos], axis=1), jnp.concatenate([-sin, sin], axis=1)


def _expand_matrix(row0):
    r = np.zeros((LANES, D_INNER), np.float32)
    for h in range(SSM_HEADS):
        r[row0 + h, h * SSM_HEAD_DIM:(h + 1) * SSM_HEAD_DIM] = 1.0
    return jnp.asarray(r, MXU_DTYPE)


def _prep_params(norm_mix, w_in, conv_w, conv_b, a_log, dt_bias, d_skip, ssm_norm,
                 w_attn_out, w_ssm_out, w_out, norm_ffn, w_gate_up, w_down, norm_final):
    b = [int(v) for v in np.cumsum([ATT_W, ATT_W, ATT_W, D_INNER, CONV_CH, 2 * SSM_HEADS, D_MODEL])]
    q, k, v = w_in[..., :b[0]], w_in[..., b[0]:b[1]], w_in[..., b[1]:b[2]]
    z, xbc, dt = w_in[..., b[2]:b[3]], w_in[..., b[3]:b[4]], w_in[..., b[4]:b[5]]
    ga, gm = w_in[..., b[5]:b[6]], w_in[..., b[6]:]
    grp = lambda t, g: t[..., g * ATT_OUT:(g + 1) * ATT_OUT]
    w_qkv = jnp.concatenate([grp(t, g) for g in range(ATT_GROUPS) for t in (q, k, v)], axis=-1)
    return {
        "norm_mix": norm_mix[:, None, :],
        "w_gates": _mxu(jnp.concatenate([z, ga, gm], axis=-1)),
        "w_xbc": _mxu(xbc),
        "w_dt": _mxu(_pad_lanes(dt, LANES)),
        "w_qkv": _mxu(w_qkv),
        "conv_w": jnp.pad(conv_w, ((0, 0), (0, SUBLANES - CONV_K), (0, 0))),
        "conv_b": conv_b[:, None, :],
        "a_log": _pad_lanes(a_log.reshape(DEPTH, 1, 2 * SSM_HEADS), LANES),
        "dt_bias": _pad_lanes(dt_bias.reshape(DEPTH, 1, 2 * SSM_HEADS), LANES),
        "d_skip": jnp.repeat(d_skip, SSM_HEAD_DIM, axis=-1)[:, None, :],
        "ssm_norm": ssm_norm[:, None, :],
        "w_attn_out": _mxu(w_attn_out),
        "w_ssm_out": _mxu(w_ssm_out),
        "w_out": _mxu(w_out),
        "norm_ffn": norm_ffn[:, None, :],
        "w_gate_up": _mxu(w_gate_up),
        "w_down": _mxu(w_down),
        "norm_final": norm_final[None, :],
        "e_fwd": _expand_matrix(0),
        "e_bwd": _expand_matrix(SSM_HEADS),
    }


def _trunk(x, p):
    batch, seq_len, _ = x.shape
    xf = x.reshape(batch * seq_len, D_MODEL)
    cos2, sin2 = _rope_tables(seq_len)
    for i in range(DEPTH):
        nw = p["norm_mix"][i]
        zg, h = _proj_gates(xf, nw, p["w_gates"][i])
        xs, bc, dt = _proj_xbc(h, p["w_xbc"][i], p["w_dt"][i], p["conv_w"][i], p["conv_b"][i],
                               p["dt_bias"][i], seq_len)
        qkv = _proj_qkv(h, p["w_qkv"][i], cos2, sin2, batch, seq_len)
        att, lse = [], []
        for g, (window, dil) in enumerate(ATT_PATTERNS):
            n = seq_len // dil
            o, l = _attn_group(qkv[g].reshape(batch * dil, 3, n, ATT_OUT), window // (2 * dil))
            att.append(o)
            lse.append(l)
        yf = _ssd_sweep(xs, bc, dt, None, None, p["a_log"][i], p["e_fwd"], p["d_skip"][i],
                        batch, seq_len, False)
        ssm = _ssd_sweep(xs, bc, dt, zg, yf, p["a_log"][i], p["e_bwd"], p["ssm_norm"][i],
                         batch, seq_len, True)
        x1 = _mix(xf, att, lse, ssm, zg, p["w_attn_out"][i], p["w_ssm_out"][i], p["w_out"][i],
                  batch, seq_len)
        xf = _ffn(x1, p["norm_ffn"][i], p["w_gate_up"][i], p["w_down"][i], p["norm_final"],
                  i == DEPTH - 1)
    return xf.reshape(batch, seq_len, D_MODEL)


def kernel(x_prompt, x_sample, norm_mix, w_in, conv_w, conv_b, a_log, dt_bias, d_skip, ssm_norm,
           w_attn_out, w_ssm_out, w_out, norm_ffn, w_gate_up, w_down, norm_final):
    p = _prep_params(norm_mix, w_in, conv_w, conv_b, a_log, dt_bias, d_skip, ssm_norm,
                     w_attn_out, w_ssm_out, w_out, norm_ffn, w_gate_up, w_down, norm_final)
    return (_trunk(x_prompt, p), _trunk(x_sample, p))
```

```python
import functools
import math

import jax
import jax.numpy as jnp
import numpy as np
from jax import lax
from jax.experimental import pallas as pl
from jax.experimental.pallas import tpu as pltpu

D_MODEL = 1024
DEPTH = 4
HEAD_DIM = 128
ATT_PATTERNS = ((128, 1), (512, 4), (2048, 16))
HEADS_PER_GROUP = 4
ATT_GROUPS = len(ATT_PATTERNS)
ATT_W = HEADS_PER_GROUP * ATT_GROUPS * HEAD_DIM
ATT_OUT = HEADS_PER_GROUP * HEAD_DIM
ROPE_THETA = 10000.0
D_INNER = 2048
SSM_HEAD_DIM = 64
SSM_HEADS = D_INNER // SSM_HEAD_DIM
SSM_GROUPS = 4
D_STATE = 128
CONV_K = 5
BC_W = 2 * SSM_GROUPS * D_STATE
CONV_CH = D_INNER + BC_W
D_FF = 2816
NORM_EPS = 1e-6

MXU_DTYPE = jnp.bfloat16
LANES = 128
SUBLANES = 8
PACKED_ROWS = 16
SSD_CHUNK = 128
SSD_STEP = 512
LOG2_E = 1.4426950408889634
GROUP_W = D_INNER // SSM_GROUPS
HEADS_PER_SSM_GROUP = SSM_HEADS // SSM_GROUPS
ATT_SUB = 128
QKV_W = 3 * ATT_OUT

TM_GATES = 1024
TM_XBC = 512
TM_QKV = 1024
TQ_ATTN = 1024
TM_MIX = 512
TM_FFN = 512

VMEM_LIMIT = 56 * 1024 * 1024


def _mxu(x):
    return x.astype(MXU_DTYPE)


def _sigmoid(x):
    return 1.0 / (1.0 + jnp.exp(-x))


def _split3(x):
    hi = _mxu(x)
    r1 = x - hi.astype(jnp.float32)
    mid = _mxu(r1)
    return hi, mid, _mxu(r1 - mid.astype(jnp.float32))


def _split_dot(lhs01, x):
    return sum(jnp.dot(lhs01, p, preferred_element_type=jnp.float32) for p in _split3(x))


def _split_dot_r(x, rhs01):
    return sum(jnp.dot(p, rhs01, preferred_element_type=jnp.float32) for p in _split3(x))


def _rms_normed(x, w):
    ms = jnp.mean(x * x, axis=-1, keepdims=True)
    return _mxu(x * lax.rsqrt(ms + NORM_EPS) * w)


def _const_spec(a, grid_rank):
    zeros = (0,) * a.ndim
    if grid_rank == 1:
        return pl.BlockSpec(a.shape, lambda i: zeros, pipeline_mode=pl.Buffered(1))
    return pl.BlockSpec(a.shape, lambda i, j: zeros, pipeline_mode=pl.Buffered(1))


GATES_CHUNK = 1024


def _proj_gates_kernel(x_ref, nw_ref, w_ref, o_ref, h_ref):
    h = _rms_normed(x_ref[...], nw_ref[...])
    h_ref[...] = h
    for c0 in range(0, w_ref.shape[1], GATES_CHUNK):
        cols = slice(c0, c0 + GATES_CHUNK)
        acc = jnp.dot(h, w_ref[:, cols], preferred_element_type=jnp.float32)
        s = _sigmoid(acc)
        o_ref[:, cols] = (acc * s if c0 < D_INNER else s).astype(o_ref.dtype)


def _proj_gates(xf, norm_w, w):
    t_total = xf.shape[0]
    tm = TM_GATES
    row = lambda width: pl.BlockSpec((tm, width), lambda i: (i, 0))
    return pl.pallas_call(
        _proj_gates_kernel,
        grid=(t_total // tm,),
        in_specs=[row(D_MODEL), _const_spec(norm_w, 1), _const_spec(w, 1)],
        out_specs=[row(w.shape[1]), row(D_MODEL)],
        out_shape=[jax.ShapeDtypeStruct((t_total, w.shape[1]), MXU_DTYPE),
                   jax.ShapeDtypeStruct((t_total, D_MODEL), MXU_DTYPE)],
        compiler_params=pltpu.CompilerParams(
            dimension_semantics=("parallel",), vmem_limit_bytes=VMEM_LIMIT),
        name="proj_gates",
    )(xf, norm_w, w)


CONV_BLOCK = 256
CONV_SEG = CONV_BLOCK // SUBLANES


def _proj_xbc_kernel(h_ref, hp_ref, hn_ref, w_ref, wdt_ref, cw_ref, cbias_ref, dtb_ref, pin_ref,
                     pout_ref, xs_ref, bc_ref, dt_ref, hh_ref, acc_ref, *, tiles_per_seq):
    tm = h_ref.shape[0]
    hr = PACKED_ROWS
    pb = CONV_BLOCK
    nb = tm // pb
    iseq = pl.program_id(0) % tiles_per_seq
    pin = pin_ref[...]
    for b in range(nb):
        blk = slice(b * pb, (b + 1) * pb)
        hh_ref[blk, :] = _mxu(jnp.dot(pin, h_ref[blk, :], preferred_element_type=jnp.float32))
    hh_ref[tm:tm + hr, :] = hp_ref[...]
    hh_ref[tm + hr:, :] = hn_ref[...]
    acc_ref[...] = jnp.dot(hh_ref[...], w_ref[...], preferred_element_type=jnp.float32)
    pout = pout_ref[...]
    s8 = SUBLANES
    for cc in range(CONV_CH // GROUP_W):
        sl = slice(cc * GROUP_W, (cc + 1) * GROUP_W)
        sub = lax.broadcasted_iota(jnp.int32, (s8, GROUP_W), 0)
        before = jnp.where(iseq > 0, acc_ref[tm + hr - s8:tm + hr, sl], 0.0)
        after = jnp.where(iseq < tiles_per_seq - 1, acc_ref[tm + hr:tm + hr + s8, sl], 0.0)
        blocks = [acc_ref[b * pb:(b + 1) * pb, sl] for b in range(nb)]
        down = lambda v, n=1: pltpu.roll(v, n, axis=0)
        up = lambda v, n=1: pltpu.roll(v, s8 - n, axis=0)
        for b in range(nb):
            cur = blocks[b]
            if b == 0:
                fill_m1, fill_m2 = down(before, 1), down(before, 2)
            else:
                fill_m1, fill_m2 = down(blocks[b - 1][pb - s8:]), down(blocks[b - 1][pb - 2 * s8:pb - s8])
            if b == nb - 1:
                fill_p0, fill_p1 = up(after, 1), up(after, 2)
            else:
                fill_p0, fill_p1 = up(blocks[b + 1][:s8]), up(blocks[b + 1][s8:2 * s8])
            e_m1 = jnp.where(sub == 0, fill_m1, down(cur[pb - s8:]))
            e_m2 = jnp.where(sub == 0, fill_m2, down(cur[pb - 2 * s8:pb - s8]))
            e_p0 = jnp.where(sub == s8 - 1, fill_p0, up(cur[:s8]))
            e_p1 = jnp.where(sub == s8 - 1, fill_p1, up(cur[s8:2 * s8]))
            ext = jnp.concatenate([e_m2, e_m1, cur, e_p0, e_p1], axis=0)
            a = cbias_ref[:, sl]
            for k in range(CONV_K):
                a = a + cw_ref[k:k + 1, sl] * ext[k * s8:k * s8 + pb]
            y = _mxu(a * _sigmoid(a))
            y = jnp.dot(pout, y, preferred_element_type=jnp.float32).astype(MXU_DTYPE)
            blk = slice(b * pb, (b + 1) * pb)
            if cc < D_INNER // GROUP_W:
                xs_ref[blk, sl] = y
            else:
                bc_ref[blk, cc * GROUP_W - D_INNER:(cc + 1) * GROUP_W - D_INNER] = y
    v = jnp.dot(h_ref[...], wdt_ref[...], preferred_element_type=jnp.float32) + dtb_ref[...]
    dt_ref[...] = jnp.maximum(v, 0.0) + jnp.log1p(jnp.exp(-jnp.abs(v)))


def _conv_permutation():
    p = np.zeros((CONV_BLOCK, CONV_BLOCK), np.float32)
    for t in range(CONV_BLOCK):
        p[(t % CONV_SEG) * SUBLANES + t // CONV_SEG, t] = 1.0
    return p


def _proj_xbc(h, w, wdt, conv_w8, conv_b, dtb, seq_len):
    t_total = h.shape[0]
    tm = TM_XBC
    hr = PACKED_ROWS
    per = tm // hr
    last = t_total // hr - 1
    p = _conv_permutation()
    pin, pout = jnp.asarray(p, MXU_DTYPE), jnp.asarray(p.T, MXU_DTYPE)
    row = lambda width: pl.BlockSpec((tm, width), lambda i: (i, 0))
    kern = functools.partial(_proj_xbc_kernel, tiles_per_seq=seq_len // tm)
    return pl.pallas_call(
        kern,
        grid=(t_total // tm,),
        in_specs=[
            row(D_MODEL),
            pl.BlockSpec((hr, D_MODEL), lambda i: (jnp.maximum(i * per - 1, 0), 0)),
            pl.BlockSpec((hr, D_MODEL), lambda i: (jnp.minimum((i + 1) * per, last), 0)),
            _const_spec(w, 1), _const_spec(wdt, 1),
            _const_spec(conv_w8, 1), _const_spec(conv_b, 1), _const_spec(dtb, 1),
            _const_spec(pin, 1), _const_spec(pout, 1),
        ],
        out_specs=[row(D_INNER), row(BC_W), row(LANES)],
        out_shape=[
            jax.ShapeDtypeStruct((t_total, D_INNER), MXU_DTYPE),
            jax.ShapeDtypeStruct((t_total, BC_W), MXU_DTYPE),
            jax.ShapeDtypeStruct((t_total, LANES), jnp.float32),
        ],
        scratch_shapes=[
            pltpu.VMEM((tm + 2 * hr, D_MODEL), MXU_DTYPE),
            pltpu.VMEM((tm + 2 * hr, CONV_CH), jnp.float32),
        ],
        compiler_params=pltpu.CompilerParams(
            dimension_semantics=("parallel",), vmem_limit_bytes=VMEM_LIMIT),
        name="proj_xbc",
    )(h, h, h, w, wdt, conv_w8, conv_b, dtb, pin, pout)


PERM_BLOCK = 256


def _proj_qkv_kernel(h_ref, w_ref, cos_ref, sin_ref, perm_ref, o0_ref, o1_ref, o2_ref, r_ref):
    tm = h_ref.shape[0]
    j = pl.program_id(1)
    n_cols = QKV_W // LANES

    def roped(c, t):
        if c < 2 * HEADS_PER_GROUP:
            return t * cos_ref[...] + pltpu.roll(t, HEAD_DIM // 2, axis=1) * sin_ref[...]
        return t

    def column_pairs():
        for cp in range(n_cols // 2):
            acc = jnp.dot(h_ref[...], w_ref[:, 2 * cp * LANES:2 * (cp + 1) * LANES],
                          preferred_element_type=jnp.float32)
            for c in (2 * cp, 2 * cp + 1):
                yield c, roped(c, acc[:, (c - 2 * cp) * LANES:(c - 2 * cp + 1) * LANES])

    @pl.when(j == 0)
    def _():
        for c, t in column_pairs():
            which, hh = divmod(c, HEADS_PER_GROUP)
            o0_ref[0, 0, which, :, hh * LANES:(hh + 1) * LANES] = _mxu(t)

    dil1 = ATT_PATTERNS[1][1]

    @pl.when(j == 1)
    def _():
        for c, t in column_pairs():
            r_ref[c] = t
            which, hh = divmod(c, HEADS_PER_GROUP)
            for r in range(dil1):
                o1_ref[0, r, which, :, hh * LANES:(hh + 1) * LANES] = _mxu(
                    r_ref[c, pl.ds(r, tm // dil1, stride=dil1), :])

    dil2 = ATT_PATTERNS[2][1]
    per_class = PERM_BLOCK // dil2

    @pl.when(j == 2)
    def _():
        perm = perm_ref[...]
        for c, t in column_pairs():
            r_ref[c] = t
        for blk in range(tm // PERM_BLOCK):
            rows = slice(blk * PERM_BLOCK, (blk + 1) * PERM_BLOCK)
            tb = jnp.concatenate([_mxu(r_ref[c, rows, :]) for c in range(n_cols)], axis=1)
            pb = _mxu(jnp.dot(perm, tb, preferred_element_type=jnp.float32))
            for r in range(dil2):
                for which in range(3):
                    o2_ref[0, r, which, blk * per_class:(blk + 1) * per_class, :] = (
                        pb[r * per_class:(r + 1) * per_class, which * ATT_OUT:(which + 1) * ATT_OUT])


def _class_permutation(dil):
    p = np.zeros((PERM_BLOCK, PERM_BLOCK), np.float32)
    per_class = PERM_BLOCK // dil
    for r in range(dil):
        for jj in range(per_class):
            p[r * per_class + jj, jj * dil + r] = 1.0
    return jnp.asarray(p, MXU_DTYPE)


def _proj_qkv(h, w, cos2, sin2, batch, seq_len):
    t_total = h.shape[0]
    tm = TM_QKV
    tps = seq_len // tm
    perm = _class_permutation(ATT_PATTERNS[2][1])
    out_specs, out_shape = [], []
    for _, dil in ATT_PATTERNS:
        out_specs.append(pl.BlockSpec((1, dil, 3, tm // dil, ATT_OUT),
                                      lambda i, j: (i // tps, 0, 0, i % tps, 0)))
        out_shape.append(jax.ShapeDtypeStruct((batch, dil, 3, seq_len // dil, ATT_OUT), MXU_DTYPE))
    return pl.pallas_call(
        _proj_qkv_kernel,
        grid=(t_total // tm, ATT_GROUPS),
        in_specs=[
            pl.BlockSpec((tm, D_MODEL), lambda i, j: (i, 0)),
            pl.BlockSpec((D_MODEL, QKV_W), lambda i, j: (0, j)),
            pl.BlockSpec((tm, HEAD_DIM), lambda i, j: (i % tps, 0)),
            pl.BlockSpec((tm, HEAD_DIM), lambda i, j: (i % tps, 0)),
            _const_spec(perm, 2),
        ],
        out_specs=out_specs,
        out_shape=out_shape,
        scratch_shapes=[pltpu.VMEM((QKV_W // LANES, tm, LANES), jnp.float32)],
        compiler_params=pltpu.CompilerParams(
            dimension_semantics=("parallel", "arbitrary"), vmem_limit_bytes=VMEM_LIMIT),
        name="proj_qkv",
    )(h, w, cos2, sin2, perm)


def _attn_kernel(q_ref, kc_ref, kp_ref, kn_ref, vc_ref, vp_ref, vn_ref, o_ref, lse_ref,
                 kbuf, vbuf, *, n, tq, radius):
    i = pl.program_id(1)
    for buf, p_ref, c_ref, n_ref in ((kbuf, kp_ref, kc_ref, kn_ref), (vbuf, vp_ref, vc_ref, vn_ref)):
        buf[0:radius, :] = p_ref[0, 0]
        buf[radius:radius + tq, :] = c_ref[0, 0]
        buf[radius + tq:, :] = n_ref[0, 0]
    sub = ATT_SUB
    win = sub + 2 * radius
    row = lax.broadcasted_iota(jnp.int32, (sub, win), 0)
    col = lax.broadcasted_iota(jnp.int32, (sub, win), 1)
    in_band = jnp.abs(col - row - radius) <= radius
    lane = lax.broadcasted_iota(jnp.int32, (sub, LANES), 1)

    def body(t, carry):
        r0 = pl.multiple_of(t * sub, sub)
        kpos = i * tq + r0 - radius + col
        in_seq = kpos.astype(jnp.uint32) < jnp.uint32(n)
        bias = jnp.where(in_band, jnp.where(in_seq, 0.0, -jnp.inf), -jnp.inf)
        lse_tile = jnp.zeros((sub, LANES), jnp.float32)
        for h in range(HEADS_PER_GROUP):
            hs = slice(h * HEAD_DIM, (h + 1) * HEAD_DIM)
            q = q_ref[0, 0, pl.ds(r0, sub), hs]
            kw = kbuf[pl.ds(r0, win), hs]
            vw = vbuf[pl.ds(r0, win), hs]
            s = lax.dot_general(q, kw, (((1,), (1,)), ((), ())), preferred_element_type=jnp.float32)
            s = s * (1.0 / math.sqrt(HEAD_DIM)) + bias
            m = jnp.max(s, axis=-1, keepdims=True)
            e = jnp.exp(s - m)
            den = jnp.sum(e, axis=-1, keepdims=True)
            o = jnp.dot(_mxu(e), vw, preferred_element_type=jnp.float32) / den
            o_ref[0, pl.ds(r0, sub), hs] = o.astype(o_ref.dtype)
            lse_tile = jnp.where(lane == h, m + jnp.log(den), lse_tile)
        lse_ref[0, pl.ds(r0, sub), :] = lse_tile
        return carry

    lax.fori_loop(0, tq // sub, body, 0, unroll=min(4, tq // sub))


def _attn_group(qkv, radius):
    s_total, _, n, _ = qkv.shape
    tq = min(TQ_ATTN, n)
    per = tq // radius
    last = n // radius - 1
    kern = functools.partial(_attn_kernel, n=n, tq=tq, radius=radius)
    cur = lambda which: pl.BlockSpec((1, 1, tq, ATT_OUT), lambda s, i: (s, which, i, 0))
    prev = lambda which: pl.BlockSpec(
        (1, 1, radius, ATT_OUT), lambda s, i: (s, which, jnp.maximum(i * per - 1, 0), 0))
    nxt = lambda which: pl.BlockSpec(
        (1, 1, radius, ATT_OUT), lambda s, i: (s, which, jnp.minimum((i + 1) * per, last), 0))
    return pl.pallas_call(
        kern,
        grid=(s_total, n // tq),
        in_specs=[cur(0), cur(1), prev(1), nxt(1), cur(2), prev(2), nxt(2)],
        out_specs=[
            pl.BlockSpec((1, tq, ATT_OUT), lambda s, i: (s, i, 0)),
            pl.BlockSpec((1, tq, LANES), lambda s, i: (s, i, 0)),
        ],
        out_shape=[
            jax.ShapeDtypeStruct((s_total, n, ATT_OUT), MXU_DTYPE),
            jax.ShapeDtypeStruct((s_total, n, LANES), jnp.float32),
        ],
        scratch_shapes=[pltpu.VMEM((tq + 2 * radius, ATT_OUT), MXU_DTYPE),
                        pltpu.VMEM((tq + 2 * radius, ATT_OUT), MXU_DTYPE)],
        compiler_params=pltpu.CompilerParams(
            dimension_semantics=("parallel", "arbitrary"), vmem_limit_bytes=VMEM_LIMIT),
        name="band_attn",
    )(qkv, qkv, qkv, qkv, qkv, qkv, qkv)


def _ssd_kernel(*refs, rev):
    if rev:
        xs_ref, bc_ref, dt_ref, sz_ref, yf_ref, alog_ref, e_ref, nw_ref, o_ref, st_ref = refs
    else:
        xs_ref, bc_ref, dt_ref, alog_ref, e_ref, dskip_ref, o_ref, st_ref = refs
    q = SSD_CHUNK

    @pl.when(pl.program_id(1) == 0)
    def _():
        st_ref[...] = jnp.zeros_like(st_ref)

    row = lax.broadcasted_iota(jnp.int32, (q, q), 0)
    col = lax.broadcasted_iota(jnp.int32, (q, q), 1)
    inside = (col >= row) if rev else (col <= row)
    tri = jnp.where(inside, 1.0, 0.0).astype(MXU_DTYPE)
    lane = lax.broadcasted_iota(jnp.int32, (q, LANES), 1)
    lane_off = SSM_HEADS if rev else 0
    e_mat = e_ref[...]
    neg_a = jnp.exp(alog_ref[...])
    expand = lambda f: jnp.dot(_mxu(f), e_mat, preferred_element_type=jnp.float32)
    chunks = range(SSD_STEP // q)
    gsls = [slice(g * GROUP_W, (g + 1) * GROUP_W) for g in range(SSM_GROUPS)]

    for ci in (reversed(chunks) if rev else chunks):
        rows = slice(ci * q, (ci + 1) * q)
        b_gs = [bc_ref[rows, g * D_STATE:(g + 1) * D_STATE] for g in range(SSM_GROUPS)]
        c_gs = [bc_ref[rows, GROUP_W + g * D_STATE:GROUP_W + (g + 1) * D_STATE]
                for g in range(SSM_GROUPS)]
        cbs = [jnp.where(inside, lax.dot_general(c_gs[g], b_gs[g], (((1,), (1,)), ((), ())),
                                                 preferred_element_type=jnp.float32), 0.0)
               for g in range(SSM_GROUPS)]
        y_offs = [jnp.dot(c_gs[g], _mxu(st_ref[:, gsls[g]]), preferred_element_type=jnp.float32)
                  for g in range(SSM_GROUPS)]
        dt = dt_ref[rows, :]
        acum = _split_dot(tri, dt * (-neg_a))
        tot = acum[0:1, :] if rev else acum[q - 1:q, :]
        acum2 = acum * LOG2_E
        acum2_t = acum2.T
        dt_t = dt.T
        e_acc = jnp.exp(acum)
        ex_acc = expand(e_acc)
        ex_w = expand(jnp.exp(tot - acum) * dt)
        edge = _split_dot_r(e_acc[0:SUBLANES, :] if rev else e_acc[q - SUBLANES:, :], e_mat)
        dec_row = edge[0:1, :] if rev else edge[SUBLANES - 1:, :]

        for g in range(SSM_GROUPS):
            gsl = gsls[g]
            x_g = xs_ref[rows, gsl].astype(jnp.float32)
            parts = []
            for jj in range(HEADS_PER_SSM_GROUP // 2):
                ms = []
                for h in (g * HEADS_PER_SSM_GROUP + 2 * jj, g * HEADS_PER_SSM_GROUP + 2 * jj + 1):
                    hl = lane_off + h
                    seg2 = jnp.minimum(acum2[:, hl:hl + 1] - acum2_t[hl:hl + 1, :], 0.0)
                    ms.append(cbs[g] * jnp.exp2(seg2) * dt_t[hl:hl + 1, :])
                m_pair = _mxu(jnp.concatenate(ms, axis=1))
                xp = x_g[:, jj * LANES:(jj + 1) * LANES]
                r_pair = _mxu(jnp.concatenate(
                    [jnp.where(lane < SSM_HEAD_DIM, xp, 0.0), jnp.where(lane >= SSM_HEAD_DIM, xp, 0.0)],
                    axis=0))
                parts.append(jnp.dot(m_pair, r_pair, preferred_element_type=jnp.float32))
            y_g = y_offs[g] * ex_acc[:, gsl] + jnp.concatenate(parts, axis=1)
            b_t = _mxu(b_gs[g].astype(jnp.float32).T)
            s_new = jnp.dot(b_t, _mxu(x_g * ex_w[:, gsl]), preferred_element_type=jnp.float32)
            st_ref[:, gsl] = st_ref[:, gsl] * dec_row[:, gsl] + s_new
            if rev:
                y = (yf_ref[rows, gsl] + y_g) * sz_ref[rows, gsl].astype(jnp.float32)
                ms_ = jnp.mean(y * y, axis=-1, keepdims=True)
                o_ref[rows, gsl] = (y * lax.rsqrt(ms_ + NORM_EPS) * nw_ref[:, gsl]).astype(o_ref.dtype)
            else:
                o_ref[rows, gsl] = y_g + x_g * dskip_ref[:, gsl]


def _ssd_sweep(xs, bc, dt, zg, yf, alog, e_mat, row_w, batch, seq_len, rev):
    t_total = batch * seq_len
    q = SSD_STEP
    nc = seq_len // q
    rowmap = (lambda b, c: b * nc + nc - 1 - c) if rev else (lambda b, c: b * nc + c)
    row = lambda width: pl.BlockSpec((q, width), lambda b, c: (rowmap(b, c), 0))
    specs = [row(D_INNER), row(BC_W), row(LANES)]
    args = [xs, bc, dt]
    if rev:
        specs += [row(D_INNER), row(D_INNER)]
        args += [zg, yf]
    specs += [_const_spec(alog, 2), _const_spec(e_mat, 2), _const_spec(row_w, 2)]
    args += [alog, e_mat, row_w]
    return pl.pallas_call(
        functools.partial(_ssd_kernel, rev=rev),
        grid=(batch, nc),
        in_specs=specs,
        out_specs=row(D_INNER),
        out_shape=jax.ShapeDtypeStruct((t_total, D_INNER), MXU_DTYPE if rev else jnp.float32),
        scratch_shapes=[pltpu.VMEM((D_STATE, D_INNER), jnp.float32)],
        compiler_params=pltpu.CompilerParams(
            dimension_semantics=("parallel", "arbitrary"), vmem_limit_bytes=VMEM_LIMIT),
        name="ssd_bwd" if rev else "ssd_fwd",
    )(*args)


def _mix_kernel(x_ref, a0_ref, a1_ref, a2_ref, l0_ref, l1_ref, l2_ref, ssm_ref, sg_ref,
                wao_ref, wso_ref, wout_ref, o_ref, as_ref, ls_ref):
    tm = x_ref.shape[0]
    ps = jnp.dot(ssm_ref[...], wso_ref[...], preferred_element_type=jnp.float32)
    for gi, (a_ref, l_ref) in enumerate(((a1_ref, l1_ref), (a2_ref, l2_ref))):
        dil = ATT_PATTERNS[gi + 1][1]
        for r in range(dil):
            rows = pl.ds(r, tm // dil, stride=dil)
            a = a_ref[0, r].astype(jnp.float32)
            for h in range(HEADS_PER_GROUP):
                as_ref[gi, h, rows, :] = a[:, h * HEAD_DIM:(h + 1) * HEAD_DIM]
            ls_ref[gi, rows, :] = l_ref[0, r]
    l0, l1, l2 = l0_ref[0, 0], ls_ref[0], ls_ref[1]
    m = jnp.maximum(jnp.maximum(l0, l1), l2)
    e0, e1, e2 = jnp.exp(l0 - m), jnp.exp(l1 - m), jnp.exp(l2 - m)
    den = e0 + e1 + e2
    al0, al1, al2 = e0 / den, e1 / den, e2 / den
    heads = []
    for h in range(HEADS_PER_GROUP):
        hs = slice(h * HEAD_DIM, (h + 1) * HEAD_DIM)
        heads.append(al0[:, h:h + 1] * a0_ref[0, 0, :, hs].astype(jnp.float32)
                     + al1[:, h:h + 1] * as_ref[0, h] + al2[:, h:h + 1] * as_ref[1, h])
    att = jnp.concatenate(heads, axis=1)
    pa = jnp.dot(_mxu(att), wao_ref[...], preferred_element_type=jnp.float32)
    mixed = (sg_ref[:, :D_MODEL].astype(jnp.float32) * pa
             + sg_ref[:, D_MODEL:].astype(jnp.float32) * ps)
    o_ref[...] = x_ref[...] + jnp.dot(_mxu(mixed), wout_ref[...], preferred_element_type=jnp.float32)


def _mix(xf, att, lse, ssm, zg, wao, wso, wout, batch, seq_len):
    t_total = xf.shape[0]
    tm = TM_MIX
    tps = seq_len // tm
    row = lambda w, blk=0: pl.BlockSpec((tm, w), lambda i: (i, blk))
    cls = lambda dil, w: pl.BlockSpec((1, dil, tm // dil, w), lambda i: (i // tps, 0, i % tps, 0))
    dils = [d for _, d in ATT_PATTERNS]
    att4 = [a.reshape(batch, d, seq_len // d, ATT_OUT) for a, d in zip(att, dils)]
    lse4 = [l.reshape(batch, d, seq_len // d, LANES) for l, d in zip(lse, dils)]
    return pl.pallas_call(
        _mix_kernel,
        grid=(t_total // tm,),
        in_specs=[row(D_MODEL)] + [cls(d, ATT_OUT) for d in dils] + [cls(d, LANES) for d in dils]
        + [row(D_INNER), row(D_INNER, 1), _const_spec(wao, 1), _const_spec(wso, 1),
           _const_spec(wout, 1)],
        out_specs=row(D_MODEL),
        out_shape=jax.ShapeDtypeStruct((t_total, D_MODEL), jnp.float32),
        scratch_shapes=[pltpu.VMEM((ATT_GROUPS - 1, HEADS_PER_GROUP, tm, HEAD_DIM), jnp.float32),
                        pltpu.VMEM((ATT_GROUPS - 1, tm, LANES), jnp.float32)],
        compiler_params=pltpu.CompilerParams(
            dimension_semantics=("parallel",), vmem_limit_bytes=VMEM_LIMIT),
        name="mix_out",
    )(xf, *att4, *lse4, ssm, zg, wao, wso, wout)


FFN_CHUNK = 1408


def _ffn_kernel(x_ref, nw_ref, wgu_ref, wd_ref, fw_ref, o_ref, *, final_norm):
    x = x_ref[...]
    h = _rms_normed(x, nw_ref[...])
    acc = x
    for c0 in range(0, D_FF, FFN_CHUNK):
        gate = jnp.dot(h, wgu_ref[:, c0:c0 + FFN_CHUNK], preferred_element_type=jnp.float32)
        up = jnp.dot(h, wgu_ref[:, D_FF + c0:D_FF + c0 + FFN_CHUNK],
                     preferred_element_type=jnp.float32)
        a = _mxu(gate * _sigmoid(gate) * up)
        acc = acc + jnp.dot(a, wd_ref[c0:c0 + FFN_CHUNK, :], preferred_element_type=jnp.float32)
    if final_norm:
        ms2 = jnp.mean(acc * acc, axis=-1, keepdims=True)
        acc = acc * lax.rsqrt(ms2 + NORM_EPS) * fw_ref[...]
    o_ref[...] = acc


def _ffn(x1, norm_w, wgu, wd, final_w, final_norm):
    t_total = x1.shape[0]
    tm = TM_FFN
    row = pl.BlockSpec((tm, D_MODEL), lambda i: (i, 0))
    return pl.pallas_call(
        functools.partial(_ffn_kernel, final_norm=final_norm),
        grid=(t_total // tm,),
        in_specs=[row, _const_spec(norm_w, 1), _const_spec(wgu, 1), _const_spec(wd, 1),
                  _const_spec(final_w, 1)],
        out_specs=row,
        out_shape=jax.ShapeDtypeStruct((t_total, D_MODEL), jnp.float32),
        compiler_params=pltpu.CompilerParams(
            dimension_semantics=("parallel",), vmem_limit_bytes=VMEM_LIMIT),
        name="ffn",
    )(x1, norm_w, wgu, wd, final_w)


def _pad_lanes(a, width):
    return jnp.pad(a, [(0, 0)] * (a.ndim - 1) + [(0, width - a.shape[-1])])


def _rope_tables(seq_len):
    half = HEAD_DIM // 2
    inv = ROPE_THETA ** (-jnp.arange(half, dtype=jnp.float32) / half)
    ang = jnp.arange(seq_len).astype(jnp.float32)[:, None] * inv[None, :]
    cos, sin = jnp.cos(ang), jnp.sin(ang)
    return jnp.concatenate([cos, cos], axis=1), jnp.concatenate([-sin, sin], axis=1)


def _expand_matrix(row0):
    r = np.zeros((LANES, D_INNER), np.float32)
    for h in range(SSM_HEADS):
        r[row0 + h, h * SSM_HEAD_DIM:(h + 1) * SSM_HEAD_DIM] = 1.0
    return jnp.asarray(r, MXU_DTYPE)


def _prep_params(norm_mix, w_in, conv_w, conv_b, a_log, dt_bias, d_skip, ssm_norm,
                 w_attn_out, w_ssm_out, w_out, norm_ffn, w_gate_up, w_down, norm_final):
    b = [int(v) for v in np.cumsum([ATT_W, ATT_W, ATT_W, D_INNER, CONV_CH, 2 * SSM_HEADS, D_MODEL])]
    q, k, v = w_in[..., :b[0]], w_in[..., b[0]:b[1]], w_in[..., b[1]:b[2]]
    z, xbc, dt = w_in[..., b[2]:b[3]], w_in[..., b[3]:b[4]], w_in[..., b[4]:b[5]]
    ga, gm = w_in[..., b[5]:b[6]], w_in[..., b[6]:]
    grp = lambda t, g: t[..., g * ATT_OUT:(g + 1) * ATT_OUT]
    w_qkv = jnp.concatenate([grp(t, g) for g in range(ATT_GROUPS) for t in (q, k, v)], axis=-1)
    return {
        "norm_mix": norm_mix[:, None, :],
        "w_gates": _mxu(jnp.concatenate([z, ga, gm], axis=-1)),
        "w_xbc": _mxu(xbc),
        "w_dt": _mxu(_pad_lanes(dt, LANES)),
        "w_qkv": _mxu(w_qkv),
        "conv_w": jnp.pad(conv_w, ((0, 0), (0, SUBLANES - CONV_K), (0, 0))),
        "conv_b": conv_b[:, None, :],
        "a_log": _pad_lanes(a_log.reshape(DEPTH, 1, 2 * SSM_HEADS), LANES),
        "dt_bias": _pad_lanes(dt_bias.reshape(DEPTH, 1, 2 * SSM_HEADS), LANES),
        "d_skip": jnp.repeat(d_skip, SSM_HEAD_DIM, axis=-1)[:, None, :],
        "ssm_norm": ssm_norm[:, None, :],
        "w_attn_out": _mxu(w_attn_out),
        "w_ssm_out": _mxu(w_ssm_out),
        "w_out": _mxu(w_out),
        "norm_ffn": norm_ffn[:, None, :],
        "w_gate_up": _mxu(w_gate_up),
        "w_down": _mxu(w_down),
        "norm_final": norm_final[None, :],
        "e_fwd": _expand_matrix(0),
        "e_bwd": _expand_matrix(SSM_HEADS),
    }


def _trunk(x, p):
    batch, seq_len, _ = x.shape
    for tile in (TM_GATES, TM_XBC, TM_QKV, TM_MIX, TM_FFN, SSD_STEP):
        assert seq_len % tile == 0, (seq_len, tile)
    assert all(seq_len // dil >= ATT_SUB for _, dil in ATT_PATTERNS), seq_len
    xf = x.reshape(batch * seq_len, D_MODEL)
    cos2, sin2 = _rope_tables(seq_len)
    for i in range(DEPTH):
        nw = p["norm_mix"][i]
        zg, h = _proj_gates(xf, nw, p["w_gates"][i])
        xs, bc, dt = _proj_xbc(h, p["w_xbc"][i], p["w_dt"][i], p["conv_w"][i], p["conv_b"][i],
                               p["dt_bias"][i], seq_len)
        qkv = _proj_qkv(h, p["w_qkv"][i], cos2, sin2, batch, seq_len)
        att, lse = [], []
        for g, (window, dil) in enumerate(ATT_PATTERNS):
            n = seq_len // dil
            o, l = _attn_group(qkv[g].reshape(batch * dil, 3, n, ATT_OUT), window // (2 * dil))
            att.append(o)
            lse.append(l)
        yf = _ssd_sweep(xs, bc, dt, None, None, p["a_log"][i], p["e_fwd"], p["d_skip"][i],
                        batch, seq_len, False)
        ssm = _ssd_sweep(xs, bc, dt, zg, yf, p["a_log"][i], p["e_bwd"], p["ssm_norm"][i],
                         batch, seq_len, True)
        x1 = _mix(xf, att, lse, ssm, zg, p["w_attn_out"][i], p["w_ssm_out"][i], p["w_out"][i],
                  batch, seq_len)
        xf = _ffn(x1, p["norm_ffn"][i], p["w_gate_up"][i], p["w_down"][i], p["norm_final"],
                  i == DEPTH - 1)
    return xf.reshape(batch, seq_len, D_MODEL)


def kernel(x_prompt, x_sample, norm_mix, w_in, conv_w, conv_b, a_log, dt_bias, d_skip, ssm_norm,
           w_attn_out, w_ssm_out, w_out, norm_ffn, w_gate_up, w_down, norm_final):
    p = _prep_params(norm_mix, w_in, conv_w, conv_b, a_log, dt_bias, d_skip, ssm_norm,
                     w_attn_out, w_ssm_out, w_out, norm_ffn, w_gate_up, w_down, norm_final)
    return (_trunk(x_prompt, p), _trunk(x_sample, p))
```

```python
import functools
import math

import jax
import jax.numpy as jnp
import numpy as np
from jax import lax
from jax.experimental import pallas as pl
from jax.experimental.pallas import tpu as pltpu

D_MODEL = 1024
DEPTH = 4
HEAD_DIM = 128
ATT_PATTERNS = ((128, 1), (512, 4), (2048, 16))
HEADS_PER_GROUP = 4
ATT_GROUPS = len(ATT_PATTERNS)
ATT_W = HEADS_PER_GROUP * ATT_GROUPS * HEAD_DIM
ATT_OUT = HEADS_PER_GROUP * HEAD_DIM
ROPE_THETA = 10000.0
D_INNER = 2048
SSM_HEAD_DIM = 64
SSM_HEADS = D_INNER // SSM_HEAD_DIM
SSM_GROUPS = 4
D_STATE = 128
CONV_K = 5
BC_W = 2 * SSM_GROUPS * D_STATE
CONV_CH = D_INNER + BC_W
D_FF = 2816
NORM_EPS = 1e-6

MXU_DTYPE = jnp.bfloat16
LANES = 128
SUBLANES = 8
PACKED_ROWS = 16
SSD_CHUNK = 128
SSD_STEP = 512
LOG2_E = 1.4426950408889634
GROUP_W = D_INNER // SSM_GROUPS
HEADS_PER_SSM_GROUP = SSM_HEADS // SSM_GROUPS
ATT_SUB = 128
QKV_W = 3 * ATT_OUT

TM_GATES = 1024
TM_XBC = 1024
TM_QKV = 1024
TQ_ATTN = 1024
TM_MIX = 512
TM_FFN = 1024

VMEM_LIMIT = 56 * 1024 * 1024


def _mxu(x):
    return x.astype(MXU_DTYPE)


def _sigmoid(x):
    return 1.0 / (1.0 + jnp.exp(-x))


def _split3(x):
    hi = _mxu(x)
    r1 = x - hi.astype(jnp.float32)
    mid = _mxu(r1)
    return hi, mid, _mxu(r1 - mid.astype(jnp.float32))


def _split_dot(lhs01, x):
    return sum(jnp.dot(lhs01, p, preferred_element_type=jnp.float32) for p in _split3(x))


def _split_dot_r(x, rhs01):
    return sum(jnp.dot(p, rhs01, preferred_element_type=jnp.float32) for p in _split3(x))


def _rms_normed(x, w):
    ms = jnp.mean(x * x, axis=-1, keepdims=True)
    return _mxu(x * lax.rsqrt(ms + NORM_EPS) * w)


def _const_spec(a, grid_rank):
    zeros = (0,) * a.ndim
    if grid_rank == 1:
        return pl.BlockSpec(a.shape, lambda i: zeros, pipeline_mode=pl.Buffered(1))
    return pl.BlockSpec(a.shape, lambda i, j: zeros, pipeline_mode=pl.Buffered(1))


GATES_CHUNK = 1024


def _proj_gates_kernel(x_ref, nw_ref, w_ref, o_ref, h_ref):
    h = _rms_normed(x_ref[...], nw_ref[...])
    h_ref[...] = h
    for c0 in range(0, w_ref.shape[1], GATES_CHUNK):
        cols = slice(c0, c0 + GATES_CHUNK)
        acc = jnp.dot(h, w_ref[:, cols], preferred_element_type=jnp.float32)
        s = _sigmoid(acc)
        o_ref[:, cols] = (acc * s if c0 < D_INNER else s).astype(o_ref.dtype)


def _proj_gates(xf, norm_w, w):
    t_total = xf.shape[0]
    tm = TM_GATES
    row = lambda width: pl.BlockSpec((tm, width), lambda i: (i, 0))
    return pl.pallas_call(
        _proj_gates_kernel,
        grid=(t_total // tm,),
        in_specs=[row(D_MODEL), _const_spec(norm_w, 1), _const_spec(w, 1)],
        out_specs=[row(w.shape[1]), row(D_MODEL)],
        out_shape=[jax.ShapeDtypeStruct((t_total, w.shape[1]), MXU_DTYPE),
                   jax.ShapeDtypeStruct((t_total, D_MODEL), MXU_DTYPE)],
        compiler_params=pltpu.CompilerParams(
            dimension_semantics=("parallel",), vmem_limit_bytes=VMEM_LIMIT),
        name="proj_gates",
    )(xf, norm_w, w)


CONV_BLOCK = 256
CONV_SEG = CONV_BLOCK // SUBLANES


def _proj_xbc_kernel(h_ref, hp_ref, hn_ref, w_ref, wdt_ref, cw_ref, cbias_ref, dtb_ref, pin_ref,
                     pout_ref, xs_ref, bc_ref, dt_ref, hh_ref, acc_ref, *, tiles_per_seq):
    tm = h_ref.shape[0]
    hr = PACKED_ROWS
    pb = CONV_BLOCK
    nb = tm // pb
    iseq = pl.program_id(0) % tiles_per_seq
    pin = pin_ref[...]
    for b in range(nb):
        blk = slice(b * pb, (b + 1) * pb)
        hh_ref[blk, :] = _mxu(jnp.dot(pin, h_ref[blk, :], preferred_element_type=jnp.float32))
    hh_ref[tm:tm + hr, :] = hp_ref[...]
    hh_ref[tm + hr:, :] = hn_ref[...]
    acc_ref[...] = jnp.dot(hh_ref[...], w_ref[...], preferred_element_type=jnp.float32)
    pout = pout_ref[...]
    s8 = SUBLANES
    for cc in range(CONV_CH // GROUP_W):
        sl = slice(cc * GROUP_W, (cc + 1) * GROUP_W)
        sub = lax.broadcasted_iota(jnp.int32, (s8, GROUP_W), 0)
        before = jnp.where(iseq > 0, acc_ref[tm + hr - s8:tm + hr, sl], 0.0)
        after = jnp.where(iseq < tiles_per_seq - 1, acc_ref[tm + hr:tm + hr + s8, sl], 0.0)
        blocks = [acc_ref[b * pb:(b + 1) * pb, sl] for b in range(nb)]
        down = lambda v, n=1: pltpu.roll(v, n, axis=0)
        up = lambda v, n=1: pltpu.roll(v, s8 - n, axis=0)
        for b in range(nb):
            cur = blocks[b]
            if b == 0:
                fill_m1, fill_m2 = down(before, 1), down(before, 2)
            else:
                fill_m1, fill_m2 = down(blocks[b - 1][pb - s8:]), down(blocks[b - 1][pb - 2 * s8:pb - s8])
            if b == nb - 1:
                fill_p0, fill_p1 = up(after, 1), up(after, 2)
            else:
                fill_p0, fill_p1 = up(blocks[b + 1][:s8]), up(blocks[b + 1][s8:2 * s8])
            e_m1 = jnp.where(sub == 0, fill_m1, down(cur[pb - s8:]))
            e_m2 = jnp.where(sub == 0, fill_m2, down(cur[pb - 2 * s8:pb - s8]))
            e_p0 = jnp.where(sub == s8 - 1, fill_p0, up(cur[:s8]))
            e_p1 = jnp.where(sub == s8 - 1, fill_p1, up(cur[s8:2 * s8]))
            ext = jnp.concatenate([e_m2, e_m1, cur, e_p0, e_p1], axis=0)
            a = cbias_ref[:, sl]
            for k in range(CONV_K):
                a = a + cw_ref[k:k + 1, sl] * ext[k * s8:k * s8 + pb]
            y = _mxu(a * _sigmoid(a))
            y = jnp.dot(pout, y, preferred_element_type=jnp.float32).astype(MXU_DTYPE)
            blk = slice(b * pb, (b + 1) * pb)
            if cc < D_INNER // GROUP_W:
                xs_ref[blk, sl] = y
            else:
                bc_ref[blk, cc * GROUP_W - D_INNER:(cc + 1) * GROUP_W - D_INNER] = y
    v = jnp.dot(h_ref[...], wdt_ref[...], preferred_element_type=jnp.float32) + dtb_ref[...]
    dt_ref[...] = jnp.maximum(v, 0.0) + jnp.log1p(jnp.exp(-jnp.abs(v)))


def _conv_permutation():
    p = np.zeros((CONV_BLOCK, CONV_BLOCK), np.float32)
    for t in range(CONV_BLOCK):
        p[(t % CONV_SEG) * SUBLANES + t // CONV_SEG, t] = 1.0
    return p


def _proj_xbc(h, w, wdt, conv_w8, conv_b, dtb, seq_len):
    t_total = h.shape[0]
    tm = TM_XBC
    hr = PACKED_ROWS
    per = tm // hr
    last = t_total // hr - 1
    p = _conv_permutation()
    pin, pout = jnp.asarray(p, MXU_DTYPE), jnp.asarray(p.T, MXU_DTYPE)
    row = lambda width: pl.BlockSpec((tm, width), lambda i: (i, 0))
    kern = functools.partial(_proj_xbc_kernel, tiles_per_seq=seq_len // tm)
    return pl.pallas_call(
        kern,
        grid=(t_total // tm,),
        in_specs=[
            row(D_MODEL),
            pl.BlockSpec((hr, D_MODEL), lambda i: (jnp.maximum(i * per - 1, 0), 0)),
            pl.BlockSpec((hr, D_MODEL), lambda i: (jnp.minimum((i + 1) * per, last), 0)),
            _const_spec(w, 1), _const_spec(wdt, 1),
            _const_spec(conv_w8, 1), _const_spec(conv_b, 1), _const_spec(dtb, 1),
            _const_spec(pin, 1), _const_spec(pout, 1),
        ],
        out_specs=[row(D_INNER), row(BC_W), row(LANES)],
        out_shape=[
            jax.ShapeDtypeStruct((t_total, D_INNER), MXU_DTYPE),
            jax.ShapeDtypeStruct((t_total, BC_W), MXU_DTYPE),
            jax.ShapeDtypeStruct((t_total, LANES), jnp.float32),
        ],
        scratch_shapes=[
            pltpu.VMEM((tm + 2 * hr, D_MODEL), MXU_DTYPE),
            pltpu.VMEM((tm + 2 * hr, CONV_CH), jnp.float32),
        ],
        compiler_params=pltpu.CompilerParams(
            dimension_semantics=("parallel",), vmem_limit_bytes=VMEM_LIMIT),
        name="proj_xbc",
    )(h, h, h, w, wdt, conv_w8, conv_b, dtb, pin, pout)


PERM_BLOCK = 256


def _proj_qkv_kernel(h_ref, w_ref, cos_ref, sin_ref, perm_ref, o0_ref, o1_ref, o2_ref, r_ref):
    tm = h_ref.shape[0]
    j = pl.program_id(1)
    n_cols = QKV_W // LANES

    def roped(c, t):
        if c < 2 * HEADS_PER_GROUP:
            return t * cos_ref[...] + pltpu.roll(t, HEAD_DIM // 2, axis=1) * sin_ref[...]
        return t

    def column_pairs():
        for cp in range(n_cols // 2):
            acc = jnp.dot(h_ref[...], w_ref[:, 2 * cp * LANES:2 * (cp + 1) * LANES],
                          preferred_element_type=jnp.float32)
            for c in (2 * cp, 2 * cp + 1):
                yield c, roped(c, acc[:, (c - 2 * cp) * LANES:(c - 2 * cp + 1) * LANES])

    @pl.when(j == 0)
    def _():
        for c, t in column_pairs():
            which, hh = divmod(c, HEADS_PER_GROUP)
            o0_ref[0, 0, which, :, hh * LANES:(hh + 1) * LANES] = _mxu(t)

    dil1 = ATT_PATTERNS[1][1]

    @pl.when(j == 1)
    def _():
        for c, t in column_pairs():
            r_ref[c] = t
            which, hh = divmod(c, HEADS_PER_GROUP)
            for r in range(dil1):
                o1_ref[0, r, which, :, hh * LANES:(hh + 1) * LANES] = _mxu(
                    r_ref[c, pl.ds(r, tm // dil1, stride=dil1), :])

    dil2 = ATT_PATTERNS[2][1]
    per_class = PERM_BLOCK // dil2

    @pl.when(j == 2)
    def _():
        perm = perm_ref[...]
        for c, t in column_pairs():
            r_ref[c] = t
        for blk in range(tm // PERM_BLOCK):
            rows = slice(blk * PERM_BLOCK, (blk + 1) * PERM_BLOCK)
            tb = jnp.concatenate([_mxu(r_ref[c, rows, :]) for c in range(n_cols)], axis=1)
            pb = _mxu(jnp.dot(perm, tb, preferred_element_type=jnp.float32))
            for r in range(dil2):
                for which in range(3):
                    o2_ref[0, r, which, blk * per_class:(blk + 1) * per_class, :] = (
                        pb[r * per_class:(r + 1) * per_class, which * ATT_OUT:(which + 1) * ATT_OUT])


def _class_permutation(dil):
    p = np.zeros((PERM_BLOCK, PERM_BLOCK), np.float32)
    per_class = PERM_BLOCK // dil
    for r in range(dil):
        for jj in range(per_class):
            p[r * per_class + jj, jj * dil + r] = 1.0
    return jnp.asarray(p, MXU_DTYPE)


def _proj_qkv(h, w, cos2, sin2, batch, seq_len):
    t_total = h.shape[0]
    tm = TM_QKV
    tps = seq_len // tm
    perm = _class_permutation(ATT_PATTERNS[2][1])
    out_specs, out_shape = [], []
    for _, dil in ATT_PATTERNS:
        out_specs.append(pl.BlockSpec((1, dil, 3, tm // dil, ATT_OUT),
                                      lambda i, j: (i // tps, 0, 0, i % tps, 0)))
        out_shape.append(jax.ShapeDtypeStruct((batch, dil, 3, seq_len // dil, ATT_OUT), MXU_DTYPE))
    return pl.pallas_call(
        _proj_qkv_kernel,
        grid=(t_total // tm, ATT_GROUPS),
        in_specs=[
            pl.BlockSpec((tm, D_MODEL), lambda i, j: (i, 0)),
            pl.BlockSpec((D_MODEL, QKV_W), lambda i, j: (0, j)),
            pl.BlockSpec((tm, HEAD_DIM), lambda i, j: (i % tps, 0)),
            pl.BlockSpec((tm, HEAD_DIM), lambda i, j: (i % tps, 0)),
            _const_spec(perm, 2),
        ],
        out_specs=out_specs,
        out_shape=out_shape,
        scratch_shapes=[pltpu.VMEM((QKV_W // LANES, tm, LANES), jnp.float32)],
        compiler_params=pltpu.CompilerParams(
            dimension_semantics=("parallel", "arbitrary"), vmem_limit_bytes=VMEM_LIMIT),
        name="proj_qkv",
    )(h, w, cos2, sin2, perm)


def _attn_kernel(q_ref, kc_ref, kp_ref, kn_ref, vc_ref, vp_ref, vn_ref, o_ref, lse_ref,
                 kbuf, vbuf, *, n, tq, radius):
    i = pl.program_id(1)
    for buf, p_ref, c_ref, n_ref in ((kbuf, kp_ref, kc_ref, kn_ref), (vbuf, vp_ref, vc_ref, vn_ref)):
        buf[0:radius, :] = p_ref[0, 0]
        buf[radius:radius + tq, :] = c_ref[0, 0]
        buf[radius + tq:, :] = n_ref[0, 0]
    sub = ATT_SUB
    win = sub + 2 * radius
    row = lax.broadcasted_iota(jnp.int32, (sub, win), 0)
    col = lax.broadcasted_iota(jnp.int32, (sub, win), 1)
    in_band = jnp.abs(col - row - radius) <= radius
    lane = lax.broadcasted_iota(jnp.int32, (sub, LANES), 1)

    def body(t, carry):
        r0 = pl.multiple_of(t * sub, sub)
        kpos = i * tq + r0 - radius + col
        in_seq = kpos.astype(jnp.uint32) < jnp.uint32(n)
        bias = jnp.where(in_band, jnp.where(in_seq, 0.0, -jnp.inf), -jnp.inf)
        lse_tile = jnp.zeros((sub, LANES), jnp.float32)
        for h in range(HEADS_PER_GROUP):
            hs = slice(h * HEAD_DIM, (h + 1) * HEAD_DIM)
            q = q_ref[0, 0, pl.ds(r0, sub), hs]
            kw = kbuf[pl.ds(r0, win), hs]
            vw = vbuf[pl.ds(r0, win), hs]
            s = lax.dot_general(q, kw, (((1,), (1,)), ((), ())), preferred_element_type=jnp.float32)
            s = s * (1.0 / math.sqrt(HEAD_DIM)) + bias
            m = jnp.max(s, axis=-1, keepdims=True)
            e = jnp.exp(s - m)
            den = jnp.sum(e, axis=-1, keepdims=True)
            o = jnp.dot(_mxu(e), vw, preferred_element_type=jnp.float32) / den
            o_ref[0, pl.ds(r0, sub), hs] = o.astype(o_ref.dtype)
            lse_tile = jnp.where(lane == h, m + jnp.log(den), lse_tile)
        lse_ref[0, pl.ds(r0, sub), :] = lse_tile
        return carry

    lax.fori_loop(0, tq // sub, body, 0, unroll=min(4, tq // sub))


def _attn_group(qkv, radius):
    s_total, _, n, _ = qkv.shape
    tq = min(TQ_ATTN, n)
    per = tq // radius
    last = n // radius - 1
    kern = functools.partial(_attn_kernel, n=n, tq=tq, radius=radius)
    cur = lambda which: pl.BlockSpec((1, 1, tq, ATT_OUT), lambda s, i: (s, which, i, 0))
    prev = lambda which: pl.BlockSpec(
        (1, 1, radius, ATT_OUT), lambda s, i: (s, which, jnp.maximum(i * per - 1, 0), 0))
    nxt = lambda which: pl.BlockSpec(
        (1, 1, radius, ATT_OUT), lambda s, i: (s, which, jnp.minimum((i + 1) * per, last), 0))
    return pl.pallas_call(
        kern,
        grid=(s_total, n // tq),
        in_specs=[cur(0), cur(1), prev(1), nxt(1), cur(2), prev(2), nxt(2)],
        out_specs=[
            pl.BlockSpec((1, tq, ATT_OUT), lambda s, i: (s, i, 0)),
            pl.BlockSpec((1, tq, LANES), lambda s, i: (s, i, 0)),
        ],
        out_shape=[
            jax.ShapeDtypeStruct((s_total, n, ATT_OUT), MXU_DTYPE),
            jax.ShapeDtypeStruct((s_total, n, LANES), jnp.float32),
        ],
        scratch_shapes=[pltpu.VMEM((tq + 2 * radius, ATT_OUT), MXU_DTYPE),
                        pltpu.VMEM((tq + 2 * radius, ATT_OUT), MXU_DTYPE)],
        compiler_params=pltpu.CompilerParams(
            dimension_semantics=("parallel", "arbitrary"), vmem_limit_bytes=VMEM_LIMIT),
        name="band_attn",
    )(qkv, qkv, qkv, qkv, qkv, qkv, qkv)


def _ssd_kernel(*refs, rev):
    if rev:
        xs_ref, bc_ref, dt_ref, sz_ref, yf_ref, alog_ref, e_ref, nw_ref, o_ref, st_ref = refs
    else:
        xs_ref, bc_ref, dt_ref, alog_ref, e_ref, dskip_ref, o_ref, st_ref = refs
    q = SSD_CHUNK

    @pl.when(pl.program_id(1) == 0)
    def _():
        st_ref[...] = jnp.zeros_like(st_ref)

    row = lax.broadcasted_iota(jnp.int32, (q, q), 0)
    col = lax.broadcasted_iota(jnp.int32, (q, q), 1)
    inside = (col >= row) if rev else (col <= row)
    tri = jnp.where(inside, 1.0, 0.0).astype(MXU_DTYPE)
    lane = lax.broadcasted_iota(jnp.int32, (q, LANES), 1)
    lane_off = SSM_HEADS if rev else 0
    e_mat = e_ref[...]
    neg_a = jnp.exp(alog_ref[...])
    expand = lambda f: jnp.dot(_mxu(f), e_mat, preferred_element_type=jnp.float32)
    chunks = range(SSD_STEP // q)
    gsls = [slice(g * GROUP_W, (g + 1) * GROUP_W) for g in range(SSM_GROUPS)]

    for ci in (reversed(chunks) if rev else chunks):
        rows = slice(ci * q, (ci + 1) * q)
        b_gs = [bc_ref[rows, g * D_STATE:(g + 1) * D_STATE] for g in range(SSM_GROUPS)]
        c_gs = [bc_ref[rows, GROUP_W + g * D_STATE:GROUP_W + (g + 1) * D_STATE]
                for g in range(SSM_GROUPS)]
        cbs = [jnp.where(inside, lax.dot_general(c_gs[g], b_gs[g], (((1,), (1,)), ((), ())),
                                                 preferred_element_type=jnp.float32), 0.0)
               for g in range(SSM_GROUPS)]
        y_offs = [jnp.dot(c_gs[g], _mxu(st_ref[:, gsls[g]]), preferred_element_type=jnp.float32)
                  for g in range(SSM_GROUPS)]
        dt = dt_ref[rows, :]
        acum = _split_dot(tri, dt * (-neg_a))
        tot = acum[0:1, :] if rev else acum[q - 1:q, :]
        acum2 = acum * LOG2_E
        acum2_t = acum2.T
        dt_t = dt.T
        e_acc = jnp.exp(acum)
        ex_acc = expand(e_acc)
        ex_w = expand(jnp.exp(tot - acum) * dt)
        edge = _split_dot_r(e_acc[0:SUBLANES, :] if rev else e_acc[q - SUBLANES:, :], e_mat)
        dec_row = edge[0:1, :] if rev else edge[SUBLANES - 1:, :]

        for g in range(SSM_GROUPS):
            gsl = gsls[g]
            x_g = xs_ref[rows, gsl].astype(jnp.float32)
            parts = []
            for jj in range(HEADS_PER_SSM_GROUP // 2):
                ms = []
                for h in (g * HEADS_PER_SSM_GROUP + 2 * jj, g * HEADS_PER_SSM_GROUP + 2 * jj + 1):
                    hl = lane_off + h
                    seg2 = jnp.minimum(acum2[:, hl:hl + 1] - acum2_t[hl:hl + 1, :], 0.0)
                    ms.append(cbs[g] * jnp.exp2(seg2) * dt_t[hl:hl + 1, :])
                m_pair = _mxu(jnp.concatenate(ms, axis=1))
                xp = x_g[:, jj * LANES:(jj + 1) * LANES]
                r_pair = _mxu(jnp.concatenate(
                    [jnp.where(lane < SSM_HEAD_DIM, xp, 0.0), jnp.where(lane >= SSM_HEAD_DIM, xp, 0.0)],
                    axis=0))
                parts.append(jnp.dot(m_pair, r_pair, preferred_element_type=jnp.float32))
            y_g = y_offs[g] * ex_acc[:, gsl] + jnp.concatenate(parts, axis=1)
            b_t = _mxu(b_gs[g].astype(jnp.float32).T)
            s_new = jnp.dot(b_t, _mxu(x_g * ex_w[:, gsl]), preferred_element_type=jnp.float32)
            st_ref[:, gsl] = st_ref[:, gsl] * dec_row[:, gsl] + s_new
            if rev:
                y = (yf_ref[rows, gsl] + y_g) * sz_ref[rows, gsl].astype(jnp.float32)
                ms_ = jnp.mean(y * y, axis=-1, keepdims=True)
                o_ref[rows, gsl] = (y * lax.rsqrt(ms_ + NORM_EPS) * nw_ref[:, gsl]).astype(o_ref.dtype)
            else:
                o_ref[rows, gsl] = y_g + x_g * dskip_ref[:, gsl]


def _ssd_sweep(xs, bc, dt, zg, yf, alog, e_mat, row_w, batch, seq_len, rev):
    t_total = batch * seq_len
    q = SSD_STEP
    nc = seq_len // q
    rowmap = (lambda b, c: b * nc + nc - 1 - c) if rev else (lambda b, c: b * nc + c)
    row = lambda width: pl.BlockSpec((q, width), lambda b, c: (rowmap(b, c), 0))
    specs = [row(D_INNER), row(BC_W), row(LANES)]
    args = [xs, bc, dt]
    if rev:
        specs += [row(D_INNER), row(D_INNER)]
        args += [zg, yf]
    specs += [_const_spec(alog, 2), _const_spec(e_mat, 2), _const_spec(row_w, 2)]
    args += [alog, e_mat, row_w]
    return pl.pallas_call(
        functools.partial(_ssd_kernel, rev=rev),
        grid=(batch, nc),
        in_specs=specs,
        out_specs=row(D_INNER),
        out_shape=jax.ShapeDtypeStruct((t_total, D_INNER), MXU_DTYPE if rev else jnp.float32),
        scratch_shapes=[pltpu.VMEM((D_STATE, D_INNER), jnp.float32)],
        compiler_params=pltpu.CompilerParams(
            dimension_semantics=("parallel", "arbitrary"), vmem_limit_bytes=VMEM_LIMIT),
        name="ssd_bwd" if rev else "ssd_fwd",
    )(*args)


def _mix_kernel(x_ref, a0_ref, a1_ref, a2_ref, l0_ref, l1_ref, l2_ref, ssm_ref, sg_ref,
                wao_ref, wso_ref, wout_ref, o_ref, as_ref, ls_ref):
    tm = x_ref.shape[0]
    ps = jnp.dot(ssm_ref[...], wso_ref[...], preferred_element_type=jnp.float32)
    for gi, (a_ref, l_ref) in enumerate(((a1_ref, l1_ref), (a2_ref, l2_ref))):
        dil = ATT_PATTERNS[gi + 1][1]
        for r in range(dil):
            rows = pl.ds(r, tm // dil, stride=dil)
            a = a_ref[0, r].astype(jnp.float32)
            for h in range(HEADS_PER_GROUP):
                as_ref[gi, h, rows, :] = a[:, h * HEAD_DIM:(h + 1) * HEAD_DIM]
            ls_ref[gi, rows, :] = l_ref[0, r]
    l0, l1, l2 = l0_ref[0, 0], ls_ref[0], ls_ref[1]
    m = jnp.maximum(jnp.maximum(l0, l1), l2)
    e0, e1, e2 = jnp.exp(l0 - m), jnp.exp(l1 - m), jnp.exp(l2 - m)
    den = e0 + e1 + e2
    al0, al1, al2 = e0 / den, e1 / den, e2 / den
    heads = []
    for h in range(HEADS_PER_GROUP):
        hs = slice(h * HEAD_DIM, (h + 1) * HEAD_DIM)
        heads.append(al0[:, h:h + 1] * a0_ref[0, 0, :, hs].astype(jnp.float32)
                     + al1[:, h:h + 1] * as_ref[0, h] + al2[:, h:h + 1] * as_ref[1, h])
    att = jnp.concatenate(heads, axis=1)
    pa = jnp.dot(_mxu(att), wao_ref[...], preferred_element_type=jnp.float32)
    mixed = (sg_ref[:, :D_MODEL].astype(jnp.float32) * pa
             + sg_ref[:, D_MODEL:].astype(jnp.float32) * ps)
    o_ref[...] = x_ref[...] + jnp.dot(_mxu(mixed), wout_ref[...], preferred_element_type=jnp.float32)


def _mix(xf, att, lse, ssm, zg, wao, wso, wout, batch, seq_len):
    t_total = xf.shape[0]
    tm = TM_MIX
    tps = seq_len // tm
    row = lambda w, blk=0: pl.BlockSpec((tm, w), lambda i: (i, blk))
    cls = lambda dil, w: pl.BlockSpec((1, dil, tm // dil, w), lambda i: (i // tps, 0, i % tps, 0))
    dils = [d for _, d in ATT_PATTERNS]
    att4 = [a.reshape(batch, d, seq_len // d, ATT_OUT) for a, d in zip(att, dils)]
    lse4 = [l.reshape(batch, d, seq_len // d, LANES) for l, d in zip(lse, dils)]
    return pl.pallas_call(
        _mix_kernel,
        grid=(t_total // tm,),
        in_specs=[row(D_MODEL)] + [cls(d, ATT_OUT) for d in dils] + [cls(d, LANES) for d in dils]
        + [row(D_INNER), row(D_INNER, 1), _const_spec(wao, 1), _const_spec(wso, 1),
           _const_spec(wout, 1)],
        out_specs=row(D_MODEL),
        out_shape=jax.ShapeDtypeStruct((t_total, D_MODEL), jnp.float32),
        scratch_shapes=[pltpu.VMEM((ATT_GROUPS - 1, HEADS_PER_GROUP, tm, HEAD_DIM), jnp.float32),
                        pltpu.VMEM((ATT_GROUPS - 1, tm, LANES), jnp.float32)],
        compiler_params=pltpu.CompilerParams(
            dimension_semantics=("parallel",), vmem_limit_bytes=VMEM_LIMIT),
        name="mix_out",
    )(xf, *att4, *lse4, ssm, zg, wao, wso, wout)


FFN_CHUNK = 1408


def _ffn_kernel(x_ref, nw_ref, wgu_ref, wd_ref, fw_ref, o_ref, *, final_norm):
    x = x_ref[...]
    h = _rms_normed(x, nw_ref[...])
    acc = x
    for c0 in range(0, D_FF, FFN_CHUNK):
        gate = jnp.dot(h, wgu_ref[:, c0:c0 + FFN_CHUNK], preferred_element_type=jnp.float32)
        up = jnp.dot(h, wgu_ref[:, D_FF + c0:D_FF + c0 + FFN_CHUNK],
                     preferred_element_type=jnp.float32)
        a = _mxu(gate * _sigmoid(gate) * up)
        acc = acc + jnp.dot(a, wd_ref[c0:c0 + FFN_CHUNK, :], preferred_element_type=jnp.float32)
    if final_norm:
        ms2 = jnp.mean(acc * acc, axis=-1, keepdims=True)
        acc = acc * lax.rsqrt(ms2 + NORM_EPS) * fw_ref[...]
    o_ref[...] = acc


def _ffn(x1, norm_w, wgu, wd, final_w, final_norm):
    t_total = x1.shape[0]
    tm = TM_FFN
    row = pl.BlockSpec((tm, D_MODEL), lambda i: (i, 0))
    return pl.pallas_call(
        functools.partial(_ffn_kernel, final_norm=final_norm),
        grid=(t_total // tm,),
        in_specs=[row, _const_spec(norm_w, 1), _const_spec(wgu, 1), _const_spec(wd, 1),
                  _const_spec(final_w, 1)],
        out_specs=row,
        out_shape=jax.ShapeDtypeStruct((t_total, D_MODEL), jnp.float32),
        compiler_params=pltpu.CompilerParams(
            dimension_semantics=("parallel",), vmem_limit_bytes=VMEM_LIMIT),
        name="ffn",
    )(x1, norm_w, wgu, wd, final_w)


def _pad_lanes(a, width):
    return jnp.pad(a, [(0, 0)] * (a.ndim - 1) + [(0, width - a.shape[-1])])


def _rope_tables(seq_len):
    half = HEAD_DIM // 2
    inv = ROPE_THETA ** (-jnp.arange(half, dtype=jnp.float32) / half)
    ang = jnp.arange(seq_len).astype(jnp.float32)[:, None] * inv[None, :]
    cos, sin = jnp.cos(ang), jnp.sin(ang)
    return jnp.concatenate([cos, cos], axis=1), jnp.concatenate([-sin, sin], axis=1)


def _expand_matrix(row0):
    r = np.zeros((LANES, D_INNER), np.float32)
    for h in range(SSM_HEADS):
        r[row0 + h, h * SSM_HEAD_DIM:(h + 1) * SSM_HEAD_DIM] = 1.0
    return jnp.asarray(r, MXU_DTYPE)


def _prep_params(norm_mix, w_in, conv_w, conv_b, a_log, dt_bias, d_skip, ssm_norm,
                 w_attn_out, w_ssm_out, w_out, norm_ffn, w_gate_up, w_down, norm_final):
    b = [int(v) for v in np.cumsum([ATT_W, ATT_W, ATT_W, D_INNER, CONV_CH, 2 * SSM_HEADS, D_MODEL])]
    q, k, v = w_in[..., :b[0]], w_in[..., b[0]:b[1]], w_in[..., b[1]:b[2]]
    z, xbc, dt = w_in[..., b[2]:b[3]], w_in[..., b[3]:b[4]], w_in[..., b[4]:b[5]]
    ga, gm = w_in[..., b[5]:b[6]], w_in[..., b[6]:]
    grp = lambda t, g: t[..., g * ATT_OUT:(g + 1) * ATT_OUT]
    w_qkv = jnp.concatenate([grp(t, g) for g in range(ATT_GROUPS) for t in (q, k, v)], axis=-1)
    return {
        "norm_mix": norm_mix[:, None, :],
        "w_gates": _mxu(jnp.concatenate([z, ga, gm], axis=-1)),
        "w_xbc": _mxu(xbc),
        "w_dt": _mxu(_pad_lanes(dt, LANES)),
        "w_qkv": _mxu(w_qkv),
        "conv_w": jnp.pad(conv_w, ((0, 0), (0, SUBLANES - CONV_K), (0, 0))),
        "conv_b": conv_b[:, None, :],
        "a_log": _pad_lanes(a_log.reshape(DEPTH, 1, 2 * SSM_HEADS), LANES),
        "dt_bias": _pad_lanes(dt_bias.reshape(DEPTH, 1, 2 * SSM_HEADS), LANES),
        "d_skip": jnp.repeat(d_skip, SSM_HEAD_DIM, axis=-1)[:, None, :],
        "ssm_norm": ssm_norm[:, None, :],
        "w_attn_out": _mxu(w_attn_out),
        "w_ssm_out": _mxu(w_ssm_out),
        "w_out": _mxu(w_out),
        "norm_ffn": norm_ffn[:, None, :],
        "w_gate_up": _mxu(w_gate_up),
        "w_down": _mxu(w_down),
        "norm_final": norm_final[None, :],
        "e_fwd": _expand_matrix(0),
        "e_bwd": _expand_matrix(SSM_HEADS),
    }


def _trunk(x, p):
    batch, seq_len, _ = x.shape
    for tile in (TM_GATES, TM_XBC, TM_QKV, TM_MIX, TM_FFN, SSD_STEP):
        assert seq_len % tile == 0, (seq_len, tile)
    assert all(seq_len // dil >= ATT_SUB for _, dil in ATT_PATTERNS), seq_len
    xf = x.reshape(batch * seq_len, D_MODEL)
    cos2, sin2 = _rope_tables(seq_len)
    for i in range(DEPTH):
        nw = p["norm_mix"][i]
        zg, h = _proj_gates(xf, nw, p["w_gates"][i])
        xs, bc, dt = _proj_xbc(h, p["w_xbc"][i], p["w_dt"][i], p["conv_w"][i], p["conv_b"][i],
                               p["dt_bias"][i], seq_len)
        qkv = _proj_qkv(h, p["w_qkv"][i], cos2, sin2, batch, seq_len)
        att, lse = [], []
        for g, (window, dil) in enumerate(ATT_PATTERNS):
            n = seq_len // dil
            o, l = _attn_group(qkv[g].reshape(batch * dil, 3, n, ATT_OUT), window // (2 * dil))
            att.append(o)
            lse.append(l)
        yf = _ssd_sweep(xs, bc, dt, None, None, p["a_log"][i], p["e_fwd"], p["d_skip"][i],
                        batch, seq_len, False)
        ssm = _ssd_sweep(xs, bc, dt, zg, yf, p["a_log"][i], p["e_bwd"], p["ssm_norm"][i],
                         batch, seq_len, True)
        x1 = _mix(xf, att, lse, ssm, zg, p["w_attn_out"][i], p["w_ssm_out"][i], p["w_out"][i],
                  batch, seq_len)
        xf = _ffn(x1, p["norm_ffn"][i], p["w_gate_up"][i], p["w_down"][i], p["norm_final"],
                  i == DEPTH - 1)
    return xf.reshape(batch, seq_len, D_MODEL)


def kernel(x_prompt, x_sample, norm_mix, w_in, conv_w, conv_b, a_log, dt_bias, d_skip, ssm_norm,
           w_attn_out, w_ssm_out, w_out, norm_ffn, w_gate_up, w_down, norm_final):
    p = _prep_params(norm_mix, w_in, conv_w, conv_b, a_log, dt_bias, d_skip, ssm_norm,
                     w_attn_out, w_ssm_out, w_out, norm_ffn, w_gate_up, w_down, norm_final)
    return (_trunk(x_prompt, p), _trunk(x_sample, p))
```
